```python
import jax, jax.numpy as jnp
from jax import lax
import numpy as np

D_MODEL = 1024
BATCH = 4
SEQ = 4096
DEPTH = 2
DEC_BATCH = 128
DEC_SEQ = 4
PAST_LEN = 2048
PAGE_SIZE = 128

W_BR = D_MODEL // 2
N_BRANCH = 3
CONV_W = 31
ATT_HD = 64
ATT_HEADS = W_BR // ATT_HD
ROPE_DIM = ATT_HD // 4
ROPE_THETA = 500000.0
DIL_PATTERNS = ((128, 1), (512, 4), (2048, 16))
WIN_MAX = max(w for w, _ in DIL_PATTERNS)
HG_DK = 128
HG_HEADS = W_BR // HG_DK
HG_DV = W_BR // HG_HEADS
HG_CHUNK = 16
EPS = 1e-6

IN_SIZES = (W_BR,) * 3 + (W_BR,) * 4 + (W_BR,) * 4 + (D_MODEL,) * N_BRANCH
IN_SPLITS = tuple(sum(IN_SIZES[:i + 1]) for i in range(len(IN_SIZES) - 1))
N_IN = sum(IN_SIZES)

kernel_name = "hybrid_conv_dilattn_hgrn2_step"


def rms_norm(x, g):
    xf = x.astype(jnp.float32)
    y = xf * lax.rsqrt(jnp.mean(xf * xf, axis=-1, keepdims=True) + EPS)
    return (y * g.astype(jnp.float32)).astype(x.dtype)


def layer_norm(x, g, b):
    xf = x.astype(jnp.float32)
    mu = jnp.mean(xf, axis=-1, keepdims=True)
    var = jnp.mean(jnp.square(xf - mu), axis=-1, keepdims=True)
    y = (xf - mu) * lax.rsqrt(var + EPS) * g.astype(jnp.float32) + b.astype(jnp.float32)
    return y.astype(x.dtype)


def partial_rope(x, pos):
    half = ROPE_DIM // 2
    freqs = jnp.power(jnp.float32(ROPE_THETA), -jnp.arange(half, dtype=jnp.float32) / half)
    ang = pos.astype(jnp.float32)[:, None] * freqs[None, :]
    cos = jnp.cos(ang)[None, :, None, :]
    sin = jnp.sin(ang)[None, :, None, :]
    x1 = x[..., :half].astype(jnp.float32)
    x2 = x[..., half:ROPE_DIM].astype(jnp.float32)
    rot = jnp.concatenate([x1 * cos - x2 * sin, x2 * cos + x1 * sin], axis=-1).astype(x.dtype)
    return jnp.concatenate([rot, x[..., ROPE_DIM:]], axis=-1)


def masked_softmax_stats(s, valid):
    m = jnp.max(jnp.where(valid, s, -jnp.inf), axis=-1, keepdims=True)
    p = jnp.where(valid, jnp.exp(s - m), 0.0)
    l = jnp.sum(p, axis=-1, keepdims=True)
    return p / l, (m + jnp.log(l))[..., 0]


def dilated_attn_prompt(q, k, v, window, dil):
    B, T, H, Dh = q.shape
    n_back = window // dil
    blk = n_back
    L = T // dil
    nb = -(-L // blk)
    Lp = nb * blk

    def by_residue(x):
        return x.reshape(B, L, dil, H, Dh)

    qs = jnp.pad(by_residue(q), ((0, 0), (0, Lp - L), (0, 0), (0, 0), (0, 0)))
    qs = qs.reshape(B, nb, blk, dil, H, Dh)

    def key_blocks(x):
        xp = jnp.pad(by_residue(x), ((0, 0), (blk, Lp - L), (0, 0), (0, 0), (0, 0)))
        xp = xp.reshape(B, nb + 1, blk, dil, H, Dh)
        return jnp.concatenate([xp[:, :-1], xp[:, 1:]], axis=2)

    kb = key_blocks(k)
    vb = key_blocks(v)
    s = jnp.einsum('bnirhd,bnjrhd->bnrhij', qs.astype(jnp.float32), kb.astype(jnp.float32)) * (ATT_HD ** -0.5)
    i = jnp.arange(blk)[:, None]
    j = jnp.arange(2 * blk)[None, :]
    dist = i + blk - j
    key_m = jnp.arange(nb)[:, None, None] * blk + j[None] - blk
    valid = (dist >= 0) & (dist <= n_back) & (key_m >= 0)
    p, lse = masked_softmax_stats(s, valid[None, :, None, None])
    o = jnp.einsum('bnrhij,bnjrhd->bnirhd', p, vb.astype(jnp.float32))
    o = o.reshape(B, Lp, dil, H, Dh)[:, :L].reshape(B, T, H, Dh)
    lse = jnp.transpose(lse, (0, 1, 4, 2, 3)).reshape(B, Lp, dil, H)[:, :L].reshape(B, T, H)
    return o, lse


def dilated_attn_decode(q, k_all, v_all, window, dil):
    B, T, H, Dh = q.shape
    buf = k_all.shape[1] - T
    n_back = window // dil
    idx = buf + jnp.arange(T)[:, None] - dil * jnp.arange(n_back + 1)[None, :]
    valid = idx >= 0
    idxc = jnp.maximum(idx, 0)
    kg = k_all[:, idxc].astype(jnp.float32)
    vg = v_all[:, idxc].astype(jnp.float32)
    s = jnp.einsum('bthd,btjhd->bthj', q.astype(jnp.float32), kg) * (ATT_HD ** -0.5)
    p, lse = masked_softmax_stats(s, valid[None, :, None, :])
    o = jnp.einsum('bthj,btjhd->bthd', p, vg)
    return o, lse


def combine_dilations(outs):
    o = jnp.stack([po for po, _ in outs])
    lse = jnp.stack([pl for _, pl in outs])
    wts = jax.nn.softmax(lse, axis=0)
    return jnp.einsum('pbth,pbthd->bthd', wts, o)


def hgrn2_chunked(q, k, v, log_f, s0):
    B, T, H, DK = q.shape
    C = HG_CHUNK
    Tp = -(-T // C) * C
    pad = ((0, 0), (0, Tp - T), (0, 0), (0, 0))

    def chunks(x):
        return jnp.pad(x, pad).reshape(B, Tp // C, C, H, x.shape[-1]).transpose(1, 0, 2, 3, 4)

    causal = jnp.tril(jnp.ones((C, C), dtype=bool))[None, :, :, None, None]

    def step(S, inp):
        qc, kc, vc, gc = inp
        G = jnp.cumsum(gc, axis=1)
        o_inter = jnp.einsum('bthk,bhkv->bthv', qc * jnp.exp(G), S)
        D = G[:, :, None] - G[:, None, :]
        decay = jnp.where(causal, jnp.exp(jnp.where(causal, D, 0.0)), 0.0)
        A = jnp.einsum('bthk,bshk,btshk->bhts', qc, kc, decay)
        o_intra = jnp.einsum('bhts,bshv->bthv', A, vc)
        G_end = G[:, -1]
        S = jnp.exp(G_end)[..., None] * S + jnp.einsum('bshk,bshv->bhkv', kc * jnp.exp(G_end[:, None] - G), vc)
        return S, o_inter + o_intra

    S, o = lax.scan(step, s0, (chunks(q), chunks(k), chunks(v), chunks(log_f)))
    o = o.transpose(1, 0, 2, 3, 4).reshape(B, Tp, H, v.shape[-1])[:, :T]
    return o, S


def mixer_layer(x, pos0, conv_buf, k_buf, v_buf, s0,
                norm_g, w_in, conv_w, conv_b, ln_g, ln_b, hg_lb, hg_norm_g, w_branch, b_gate, w_out):
    B, T, _ = x.shape
    h = rms_norm(x, norm_g)
    proj = jnp.einsum('btd,dn->btn', h, w_in)
    (a_val, a_glu, a_z, b_q, b_k, b_v, b_z, c_q, c_f, c_i, c_z, g_a, g_b, g_c) = jnp.split(proj, IN_SPLITS, axis=-1)

    u = a_val * jax.nn.sigmoid(a_glu)
    u_all = jnp.concatenate([conv_buf.astype(u.dtype), u], axis=1)
    conv = lax.conv_general_dilated(u_all, conv_w.astype(u.dtype)[:, None, :], (1,), 'VALID',
                                    dimension_numbers=('NWC', 'WIO', 'NWC'),
                                    feature_group_count=W_BR) + conv_b
    y_a = jax.nn.silu(layer_norm(conv, ln_g, ln_b)) * jax.nn.silu(a_z)
    new_conv = u_all[:, u_all.shape[1] - (CONV_W - 1):]

    pos = pos0 + jnp.arange(T, dtype=jnp.int32)
    q = partial_rope(b_q.reshape(B, T, ATT_HEADS, ATT_HD), pos)
    k = partial_rope(b_k.reshape(B, T, ATT_HEADS, ATT_HD), pos)
    v = b_v.reshape(B, T, ATT_HEADS, ATT_HD)
    if k_buf is None:
        outs = [dilated_attn_prompt(q, k, v, w, r) for (w, r) in DIL_PATTERNS]
        keep = min(WIN_MAX, T)
        new_k = k[:, T - keep:]
        new_v = v[:, T - keep:]
    else:
        k_all = jnp.concatenate([k_buf.astype(k.dtype), k], axis=1)
        v_all = jnp.concatenate([v_buf.astype(v.dtype), v], axis=1)
        outs = [dilated_attn_decode(q, k_all, v_all, w, r) for (w, r) in DIL_PATTERNS]
        n_all = k_all.shape[1]
        keep = min(WIN_MAX, n_all)
        new_k = k_all[:, n_all - keep:]
        new_v = v_all[:, n_all - keep:]
    o_b = combine_dilations(outs)
    y_b = o_b.reshape(B, T, W_BR).astype(x.dtype) * jax.nn.silu(b_z)

    fp = c_f.astype(jnp.float32)
    lb = hg_lb.astype(jnp.float32)
    log_f = jnp.logaddexp(jnp.log(lb), jnp.log1p(-lb) + jax.nn.log_sigmoid(fp))
    kk = (1.0 - lb) * jax.nn.sigmoid(-fp)
    qq = jax.nn.silu(c_q.astype(jnp.float32)) * (HG_DK ** -0.5)
    vv = c_i.astype(jnp.float32)
    o_c, s_new = hgrn2_chunked(qq.reshape(B, T, HG_HEADS, HG_DK), kk.reshape(B, T, HG_HEADS, HG_DK),
                               vv.reshape(B, T, HG_HEADS, HG_DV), log_f.reshape(B, T, HG_HEADS, HG_DK),
                               s0.astype(jnp.float32))
    o_c = rms_norm(o_c, hg_norm_g.reshape(HG_HEADS, HG_DV))
    y_c = o_c.reshape(B, T, W_BR).astype(x.dtype) * jax.nn.silu(c_z)

    ys = jnp.stack([y_a, y_b, y_c])
    gates = jax.nn.sigmoid(jnp.stack([g_a, g_b, g_c]) + b_gate[:, None, None, :])
    branch_out = jnp.einsum('pbtw,pwd->pbtd', ys, w_branch)
    merged = jnp.sum(gates * branch_out, axis=0)
    x = x + jnp.einsum('btd,de->bte', merged, w_out)
    return x, new_conv, new_k, new_v, s_new


def setup_inputs(seed: int = 0) -> dict:
    key = jax.random.key(seed)
    ks = jax.random.split(key, 20)

    def nrm(k, shape, scale):
        return jax.random.normal(k, shape, jnp.float32) * scale

    buf = min(WIN_MAX, PAST_LEN)
    return {
        "x_prompt": nrm(ks[0], (BATCH, SEQ, D_MODEL), 1.0),
        "x_sample": nrm(ks[1], (DEC_BATCH, DEC_SEQ, D_MODEL), 1.0),
        "cache_conv": nrm(ks[2], (DEPTH, DEC_BATCH, CONV_W - 1, W_BR), 1.0),
        "cache_win_k": nrm(ks[3], (DEPTH, DEC_BATCH, buf, ATT_HEADS, ATT_HD), 1.0),
        "cache_win_v": nrm(ks[4], (DEPTH, DEC_BATCH, buf, ATT_HEADS, ATT_HD), 1.0),
        "state_hgrn": nrm(ks[5], (DEPTH, DEC_BATCH, HG_HEADS, HG_DK, HG_DV), 0.5),
        "norm_g": 1.0 + nrm(ks[6], (DEPTH, D_MODEL), 0.02),
        "w_in": nrm(ks[7], (DEPTH, D_MODEL, N_IN), D_MODEL ** -0.5),
        "conv_w": nrm(ks[8], (DEPTH, CONV_W, W_BR), CONV_W ** -0.5),
        "conv_b": nrm(ks[9], (DEPTH, W_BR), 0.02),
        "ln_a_g": 1.0 + nrm(ks[10], (DEPTH, W_BR), 0.02),
        "ln_a_b": nrm(ks[11], (DEPTH, W_BR), 0.02),
        "hg_lb_param": nrm(ks[12], (DEPTH, W_BR), 0.5),
        "hg_norm_g": 1.0 + nrm(ks[13], (DEPTH, W_BR), 0.02),
        "w_branch": nrm(ks[14], (DEPTH, N_BRANCH, W_BR, D_MODEL), W_BR ** -0.5),
        "b_gate": nrm(ks[15], (DEPTH, N_BRANCH, D_MODEL), 0.01),
        "w_out": nrm(ks[16], (DEPTH, D_MODEL, D_MODEL), D_MODEL ** -0.5),
        "final_norm_g": 1.0 + nrm(ks[17], (D_MODEL,), 0.02),
    }


def reference(x_prompt, x_sample, cache_conv, cache_win_k, cache_win_v, state_hgrn,
              norm_g, w_in, conv_w, conv_b, ln_a_g, ln_a_b, hg_lb_param, hg_norm_g,
              w_branch, b_gate, w_out, final_norm_g):
    lb_all = jnp.cumsum(jax.nn.softmax(hg_lb_param.astype(jnp.float32), axis=0), axis=0)
    lb_all = lb_all - lb_all[:1]

    yp, ys = x_prompt, x_sample
    conv_p, k_p, v_p, s_p = [], [], [], []
    conv_s, k_s, v_s, s_s = [], [], [], []
    for l in range(DEPTH):
        w = (norm_g[l], w_in[l], conv_w[l], conv_b[l], ln_a_g[l], ln_a_b[l], lb_all[l],
             hg_norm_g[l], w_branch[l], b_gate[l], w_out[l])
        zero_conv = jnp.zeros((yp.shape[0], CONV_W - 1, W_BR), yp.dtype)
        zero_s = jnp.zeros((yp.shape[0], HG_HEADS, HG_DK, HG_DV), jnp.float32)
        yp, c1, k1, v1, s1 = mixer_layer(yp, 0, zero_conv, None, None, zero_s, *w)
        ys, c2, k2, v2, s2 = mixer_layer(ys, PAST_LEN, cache_conv[l], cache_win_k[l], cache_win_v[l],
                                         state_hgrn[l], *w)
        conv_p.append(c1); k_p.append(k1); v_p.append(v1); s_p.append(s1)
        conv_s.append(c2); k_s.append(k2); v_s.append(v2); s_s.append(s2)

    y_prompt = rms_norm(yp, final_norm_g)
    y_sample = rms_norm(ys, final_norm_g)
    return (y_prompt, y_sample,
            jnp.stack(conv_p), jnp.stack(k_p), jnp.stack(v_p), jnp.stack(s_p),
            jnp.stack(conv_s), jnp.stack(k_s), jnp.stack(v_s), jnp.stack(s_s))
```

```python
import functools

import jax
import jax.numpy as jnp
from jax import lax
from jax.experimental import pallas as pl
from jax.experimental.pallas import tpu as pltpu

F32 = jnp.float32
BF16 = jnp.bfloat16

D_MODEL = 1024
W_BR = 512
CONV_W = 31
ATT_HD = 64
ATT_HEADS = 8
ROPE_DIM = 16
ROPE_THETA = 500000.0
DIL_PATTERNS = ((128, 1), (512, 4), (2048, 16))
N_BACK = 128
WIN_MAX = 2048
PAST_LEN = 2048
HG_DK = 128
HG_DV = 128
HG_HEADS = 4
EPS = 1e-6
NEG = -1e30

LANES = 128
SUBLANES = 8
VMEM_LIMIT = 56 * 1024 * 1024


def _params(sem):
    return pltpu.CompilerParams(dimension_semantics=sem, vmem_limit_bytes=VMEM_LIMIT)


def _sigmoid(x):
    return 1.0 / (1.0 + jnp.exp(-x))


def _dot(a, b):
    return jnp.dot(a, b, preferred_element_type=F32)


def _dot_nt(a, b):
    return lax.dot_general(a, b, (((1,), (1,)), ((), ())), preferred_element_type=F32)


def _rms(x, g):
    return x * lax.rsqrt(jnp.mean(x * x, axis=-1, keepdims=True) + EPS) * g


def _inproj_kernel(x_ref, g_ref, w_ref, cos_ref, sa_ref, sb_ref, lbp_ref,
                   u_ref, q_ref, k_ref, v_ref, qq_ref, kk_ref, lf_ref, vv_ref):
    hb = _rms(x_ref[...], g_ref[...]).astype(BF16)

    def proj(c):
        return _dot(hb, w_ref[:, c * W_BR:(c + 1) * W_BR])

    u_ref[...] = proj(0) * _sigmoid(proj(1))

    cos = cos_ref[...]
    sa = sa_ref[...]
    sb = sb_ref[...]

    def rope_to(dst_ref, y):
        for gi in range(W_BR // LANES):
            yg = y[:, gi * LANES:(gi + 1) * LANES]
            dst_ref[:, gi * LANES:(gi + 1) * LANES] = (
                yg * cos + pltpu.roll(yg, LANES - ROPE_DIM // 2, 1) * sa + pltpu.roll(yg, ROPE_DIM // 2, 1) * sb)

    rope_to(q_ref, proj(2))
    rope_to(k_ref, proj(3))
    v_ref[...] = proj(4)

    cq = proj(5)
    qq_ref[...] = cq * _sigmoid(cq) * (HG_DK ** -0.5)

    fp = proj(6)
    log_lb = lbp_ref[0:1, :]
    log_1m_lb = lbp_ref[1:2, :]
    one_m_lb = lbp_ref[2:3, :]
    log_sig = jnp.minimum(fp, 0.0) - jnp.log1p(jnp.exp(-jnp.abs(fp)))
    b = log_1m_lb + log_sig
    lf_ref[...] = jnp.maximum(log_lb, b) + jnp.log1p(jnp.exp(-jnp.abs(log_lb - b)))
    kk_ref[...] = one_m_lb * _sigmoid(-fp)
    vv_ref[...] = proj(7)


def _inproj(x2d, norm_g, w1, cos, sa, sb, lbp, tm):
    n = x2d.shape[0]
    tm = min(tm, n)
    nrope = cos.shape[0] // tm
    row = lambda i: (i, 0)
    fixed = lambda i: (0, 0)
    rope_spec = pl.BlockSpec((tm, LANES), lambda i: (i % nrope, 0))
    return pl.pallas_call(
        _inproj_kernel,
        grid=(n // tm,),
        in_specs=[pl.BlockSpec((tm, D_MODEL), row), pl.BlockSpec((1, D_MODEL), fixed),
                  pl.BlockSpec(w1.shape, fixed), rope_spec, rope_spec, rope_spec,
                  pl.BlockSpec(lbp.shape, fixed)],
        out_specs=[pl.BlockSpec((tm, W_BR), row)] * 8,
        out_shape=[jax.ShapeDtypeStruct((n, W_BR), F32)] * 8,
        compiler_params=_params(("parallel",)),
        name="inproj",
    )(x2d, norm_g, w1, cos, sa, sb, lbp)


CONV_HALO = 32


def _ln_swish(y, lg, lb):
    mu = jnp.mean(y, axis=-1, keepdims=True)
    d = y - mu
    var = jnp.mean(d * d, axis=-1, keepdims=True)
    yn = d * lax.rsqrt(var + EPS) * lg + lb
    return yn * _sigmoid(yn)


def _conv_prompt_kernel(cur_ref, halo_ref, w_ref, b_ref, lg_ref, lb_ref, o_ref, ext_ref, *, tt, rc):
    i = pl.program_id(1)
    ext_ref[0:CONV_HALO, :] = jnp.where(i > 0, halo_ref[0], 0.0)
    ext_ref[CONV_HALO:CONV_HALO + tt, :] = cur_ref[0]
    off = CONV_HALO - (CONV_W - 1)

    def body(c, carry):
        r0 = pl.multiple_of(c * rc, rc)
        acc = jnp.zeros((rc, W_BR), F32)
        for r in range(SUBLANES):
            nrows = rc if r == 0 else rc + SUBLANES
            y = None
            for m in range((CONV_W + off) // SUBLANES + 1):
                j = SUBLANES * m + r - off
                if 0 <= j < CONV_W:
                    term = w_ref[j:j + 1, :] * ext_ref[pl.ds(r0 + SUBLANES * m, nrows), :]
                    y = term if y is None else y + term
            acc = acc + y[r:r + rc]
        o_ref[0, pl.ds(r0, rc), :] = _ln_swish(acc + b_ref[...], lg_ref[...], lb_ref[...])
        return carry

    lax.fori_loop(0, tt // rc, body, 0)


def _conv_prompt(u, conv_w, conv_b, ln_g, ln_b, tt=512, rc=32):
    b, t, _ = u.shape
    per_tile = tt // CONV_HALO
    fixed = lambda bi, i: (0, 0)
    return pl.pallas_call(
        functools.partial(_conv_prompt_kernel, tt=tt, rc=rc),
        grid=(b, t // tt),
        in_specs=[pl.BlockSpec((1, tt, W_BR), lambda bi, i: (bi, i, 0)),
                  pl.BlockSpec((1, CONV_HALO, W_BR), lambda bi, i: (bi, jnp.maximum(i * per_tile - 1, 0), 0)),
                  pl.BlockSpec((CONV_W, W_BR), fixed), pl.BlockSpec((1, W_BR), fixed),
                  pl.BlockSpec((1, W_BR), fixed), pl.BlockSpec((1, W_BR), fixed)],
        out_specs=pl.BlockSpec((1, tt, W_BR), lambda bi, i: (bi, i, 0)),
        out_shape=jax.ShapeDtypeStruct((b, t, W_BR), F32),
        scratch_shapes=[pltpu.VMEM((CONV_HALO + tt, W_BR), F32)],
        compiler_params=_params(("parallel", "parallel")),
        name="conv_prompt",
    )(u, u, conv_w, conv_b, ln_g, ln_b)


def _conv_decode_kernel(cache_ref, u_ref, w_ref, b_ref, lg_ref, lb_ref, y_ref, nc_ref, *, ts):
    hist = CONV_W - 1

    def slab(r):
        return cache_ref[0, r] if r < hist else u_ref[r - hist]

    for t in range(ts):
        acc = w_ref[0:1, :] * slab(t)
        for j in range(1, CONV_W):
            acc = acc + w_ref[j:j + 1, :] * slab(t + j)
        y_ref[t] = _ln_swish(acc + b_ref[...], lg_ref[...], lb_ref[...])
    for r in range(hist):
        nc_ref[r] = slab(r + ts)


def _conv_decode(cache_t, layer, u_t, conv_w, conv_b, ln_g, ln_b, bb=32):
    ts, bs, _ = u_t.shape
    bb = min(bb, bs)
    hist = CONV_W - 1
    fixed = lambda i: (0, 0)
    return pl.pallas_call(
        functools.partial(_conv_decode_kernel, ts=ts),
        grid=(bs // bb,),
        in_specs=[pl.BlockSpec((1, hist, bb, W_BR), lambda i: (layer, 0, i, 0)),
                  pl.BlockSpec((ts, bb, W_BR), lambda i: (0, i, 0)),
                  pl.BlockSpec((CONV_W, W_BR), fixed), pl.BlockSpec((1, W_BR), fixed),
                  pl.BlockSpec((1, W_BR), fixed), pl.BlockSpec((1, W_BR), fixed)],
        out_specs=[pl.BlockSpec((ts, bb, W_BR), lambda i: (0, i, 0)),
                   pl.BlockSpec((hist, bb, W_BR), lambda i: (0, i, 0))],
        out_shape=[jax.ShapeDtypeStruct((ts, bs, W_BR), F32),
                   jax.ShapeDtypeStruct((hist, bs, W_BR), F32)],
        compiler_params=_params(("parallel",)),
        name="conv_decode",
    )(cache_t, u_t, conv_w, conv_b, ln_g, ln_b)


ATT_BLK = 128


def _attn_prompt_kernel(q_ref, k_ref, v_ref, o_ref, os_ref, ls_ref, *, t, rows_c):
    r = lax.broadcasted_iota(jnp.int32, (ATT_BLK, ATT_BLK), 0)
    c = lax.broadcasted_iota(jnp.int32, (ATT_BLK, ATT_BLK), 1)
    own_ok = c <= r
    prev_ok = c >= r
    lane = lax.broadcasted_iota(jnp.int32, (ATT_BLK, LANES), 1)
    first_head = lane < ATT_HD
    scale = ATT_HD ** -0.5

    for p, (_, dil) in enumerate(DIL_PATTERNS):
        nb = t // dil // ATT_BLK
        span = dil * ATT_BLK

        def rows(start, dil=dil):
            if dil == 1:
                return pl.ds(pl.multiple_of(start, ATT_BLK), ATT_BLK)
            return pl.ds(start, ATT_BLK, stride=dil)

        def body(idx, carry, p=p, dil=dil, nb=nb, span=span, rows=rows):
            cls = idx // nb
            n = idx - cls * nb
            start = cls + span * n
            has_prev = n > 0
            pstart = jnp.where(has_prev, start - span, start)
            qb = q_ref[0, rows(start), :] * scale
            ko = k_ref[0, rows(start), :].astype(BF16)
            kp = k_ref[0, rows(pstart), :].astype(BF16)
            vo = v_ref[0, rows(start), :].astype(BF16)
            vp = v_ref[0, rows(pstart), :].astype(BF16)
            pmask = jnp.logical_and(prev_ok, has_prev)
            outs, lses = [], []
            for hh in range(2):
                head = first_head if hh == 0 else jnp.logical_not(first_head)
                qh = jnp.where(head, qb, 0.0).astype(BF16)
                so = jnp.where(own_ok, _dot_nt(qh, ko), NEG)
                sp = jnp.where(pmask, _dot_nt(qh, kp), NEG)
                m = jnp.maximum(jnp.max(so, axis=-1, keepdims=True), jnp.max(sp, axis=-1, keepdims=True))
                po = jnp.exp(so - m)
                pp = jnp.exp(sp - m)
                l = jnp.sum(po, axis=-1, keepdims=True) + jnp.sum(pp, axis=-1, keepdims=True)
                acc = _dot(po.astype(BF16), vo) + _dot(pp.astype(BF16), vp)
                outs.append(acc / l)
                lses.append(m + jnp.log(l))
            os_ref[p, rows(start), :] = jnp.where(first_head, outs[0], outs[1])
            ls_ref[p, rows(start), :] = jnp.where(first_head, lses[0], lses[1])
            return carry

        lax.fori_loop(0, dil * nb, body, 0)

    def combine(i, carry):
        r0 = pl.multiple_of(i * rows_c, rows_c)
        ls = [ls_ref[p, pl.ds(r0, rows_c), :] for p in range(3)]
        mx = jnp.maximum(jnp.maximum(ls[0], ls[1]), ls[2])
        ws = [jnp.exp(x - mx) for x in ls]
        num = ws[0] * os_ref[0, pl.ds(r0, rows_c), :]
        num = num + ws[1] * os_ref[1, pl.ds(r0, rows_c), :]
        num = num + ws[2] * os_ref[2, pl.ds(r0, rows_c), :]
        o_ref[0, pl.ds(r0, rows_c), :] = num / (ws[0] + ws[1] + ws[2])
        return carry

    lax.fori_loop(0, t // rows_c, combine, 0)


def _attn_prompt(q, k, v, rows_c=256):
    b, t, _ = q.shape
    spec = pl.BlockSpec((1, t, LANES), lambda bi, hp: (bi, 0, hp))
    return pl.pallas_call(
        functools.partial(_attn_prompt_kernel, t=t, rows_c=rows_c),
        grid=(b, W_BR // LANES),
        in_specs=[spec, spec, spec],
        out_specs=spec,
        out_shape=jax.ShapeDtypeStruct((b, t, W_BR), F32),
        scratch_shapes=[pltpu.VMEM((3, t, LANES), F32), pltpu.VMEM((3, t, LANES), F32)],
        compiler_params=_params(("parallel", "parallel")),
        name="attn_prompt",
    )(q, k, v)


DEC_Q = 8
DEC_SROWS = 32
DEC_HB = 4


def _attn_decode_kernel(q_ref, kn_ref, vn_ref, kc_ref, vc_ref, bias_ref, *rest, ts, aliased):
    if aliased:
        rest = rest[2:]
    o_ref, ko_ref, vo_ref = rest
    nrow = kc_ref.shape[-1]
    ext = nrow + LANES
    npat = len(DIL_PATTERNS)
    zl = jnp.zeros((DEC_Q, LANES - ATT_HD), F32)
    zr = jnp.zeros((LANES - DEC_Q, LANES), F32)
    bias = bias_ref[...]

    def new_t(x):
        x = jnp.concatenate([jnp.concatenate([x, zl], axis=1), zr], axis=0)
        return x.T[0:ATT_HD, :]

    for hh in range(DEC_HB):
        kext = jnp.concatenate([kc_ref[0, 0, hh], new_t(kn_ref[0, hh])], axis=1)
        vext = jnp.concatenate([vc_ref[0, 0, hh], new_t(vn_ref[0, hh])], axis=1)
        qh = (q_ref[0, hh] * (ATT_HD ** -0.5)).astype(BF16)
        s = _dot(qh, kext.astype(BF16)) + bias
        m = jnp.max(s, axis=-1, keepdims=True)
        pexp = jnp.exp(s - m)
        l = jnp.sum(pexp, axis=-1, keepdims=True)
        acc = _dot_nt(pexp.astype(BF16), vext.astype(BF16))
        mx = jnp.maximum(jnp.maximum(m[0:DEC_Q], m[DEC_Q:2 * DEC_Q]), m[2 * DEC_Q:3 * DEC_Q])
        num = jnp.zeros((DEC_Q, ATT_HD), F32)
        den = jnp.zeros((DEC_Q, 1), F32)
        for p in range(npat):
            w = jnp.exp(m[p * DEC_Q:(p + 1) * DEC_Q] - mx)
            num = num + w * acc[p * DEC_Q:(p + 1) * DEC_Q]
            den = den + w * l[p * DEC_Q:(p + 1) * DEC_Q]
        o_ref[0, hh] = num / den
        ko_ref[0, 0, hh] = pltpu.roll(kext, ext - ts, 1)[:, 0:nrow]
        vo_ref[0, 0, hh] = pltpu.roll(vext, ext - ts, 1)[:, 0:nrow]


def _attn_decode(q3, kn, vn, cache_kt, cache_vt, bias, layer, prev_out, ts):
    bs = q3.shape[0]
    nrow = cache_kt.shape[-1]
    cache = pl.BlockSpec((1, 1, DEC_HB, ATT_HD, nrow), lambda b, g: (layer, b, g, 0, 0))
    new = pl.BlockSpec((1, DEC_HB, DEC_Q, ATT_HD), lambda b, g: (b, g, 0, 0))
    in_specs = [pl.BlockSpec((1, DEC_HB, DEC_SROWS, ATT_HD), lambda b, g: (b, g, 0, 0)), new, new, cache, cache,
                pl.BlockSpec(bias.shape, lambda b, g: (0, 0))]
    args = [q3, kn, vn, cache_kt, cache_vt, bias]
    aliases = {}
    if prev_out is not None:
        in_specs += [pl.BlockSpec(memory_space=pl.ANY)] * 2
        aliases = {len(args): 1, len(args) + 1: 2}
        args += list(prev_out)
    return pl.pallas_call(
        functools.partial(_attn_decode_kernel, ts=ts, aliased=prev_out is not None),
        grid=(bs, ATT_HEADS // DEC_HB),
        in_specs=in_specs,
        out_specs=[new, cache, cache],
        out_shape=[jax.ShapeDtypeStruct((bs, ATT_HEADS, DEC_Q, ATT_HD), F32),
                   jax.ShapeDtypeStruct(cache_kt.shape, F32), jax.ShapeDtypeStruct(cache_vt.shape, F32)],
        input_output_aliases=aliases,
        compiler_params=_params(("parallel", "parallel")),
        name="attn_decode",
    )(*args)


def _decode_bias(nrow, ts):
    ri = jnp.arange(DEC_SROWS)
    pat, qt = ri // DEC_Q, ri % DEC_Q
    wins = jnp.array([w for w, _ in DIL_PATTERNS] + [0], jnp.int32)[pat]
    dils = jnp.array([d for _, d in DIL_PATTERNS] + [1], jnp.int32)[pat]
    real_row = (pat < len(DIL_PATTERNS)) & (qt < ts)
    ci = jnp.arange(nrow + LANES)
    key_real = ci < nrow + ts
    delta = (nrow + qt)[:, None] - ci[None, :]
    ok = (delta >= 0) & (delta % dils[:, None] == 0) & (delta <= wins[:, None]) & key_real[None, :]
    ok = ok | ~real_row[:, None]
    return jnp.where(ok, 0.0, NEG).astype(F32)


HG_TILE = 128


def _hgrn_kernel(q_ref, k_ref, g_ref, v_ref, s0_ref, ng_ref, o_ref, so_ref, st_ref, *, seg):
    c = HG_TILE
    nseg = c // seg
    t = pl.program_id(1)

    @pl.when(t == 0)
    def _():
        for j in range(nseg):
            for h in range(HG_HEADS):
                st_ref[j, h] = s0_ref[0, j, h].T

    q = q_ref[...]
    k = k_ref[...]
    g = g_ref[...]
    v = v_ref[...]
    row = lax.broadcasted_iota(jnp.int32, (c, c), 0)
    col = lax.broadcasted_iota(jnp.int32, (c, c), 1)
    rowi = lax.broadcasted_iota(jnp.int32, (c, W_BR), 0)
    seg_shift = seg.bit_length() - 1
    same_seg = (row >> seg_shift) == (col >> seg_shift)

    tri = jnp.where(jnp.logical_and(row >= col, same_seg), 1.0, 0.0).astype(BF16)
    g_hi = g.astype(BF16)
    r1 = g - g_hi.astype(F32)
    g_mid = r1.astype(BF16)
    g_lo = (r1 - g_mid.astype(F32)).astype(BF16)
    gc = _dot(tri, g_hi) + _dot(tri, g_mid) + _dot(tri, g_lo)

    def seg_last(x):
        x3 = x.reshape(nseg, seg, W_BR)
        return jnp.broadcast_to(x3[:, seg - 1:seg, :], x3.shape).reshape(c, W_BR)

    g_end = seg_last(gc)
    q_dec = (q * jnp.exp(gc)).astype(BF16)
    k_dec = (k * jnp.exp(g_end - gc)).astype(BF16)
    vb = v.astype(BF16)

    def head(x, h):
        return x[:, h * HG_DK:(h + 1) * HG_DK]

    qb = q.astype(BF16)
    kb = k.astype(BF16)
    a_acc = [jnp.where(row == col, _dot_nt(head(qb, h), head(kb, h)), 0.0) for h in range(HG_HEADS)]

    hlev = 1
    while hlev < seg:
        blk = 2 * hlev
        if blk >= SUBLANES:
            g3 = gc.reshape(c // blk, blk, W_BR)
            ref = jnp.broadcast_to(g3[:, hlev - 1:hlev, :], g3.shape).reshape(c, W_BR)
        elif hlev == 1:
            ref = jnp.where((rowi & 1) == 1, pltpu.roll(gc, 1, 0), gc)
        else:
            pos = rowi & 3
            ref = jnp.where(pos == 0, pltpu.roll(gc, c - 1, 0),
                            jnp.where(pos == 1, gc, jnp.where(pos == 2, pltpu.roll(gc, 1, 0), pltpu.roll(gc, 2, 0))))
        x = jnp.exp(-jnp.abs(gc - ref))
        upper = (rowi & (blk - 1)) >= hlev
        q_lv = jnp.where(upper, q * x, 0.0).astype(BF16)
        k_lv = jnp.where(upper, 0.0, k * x).astype(BF16)
        shift = blk.bit_length() - 1
        same_blk = (row >> shift) == (col >> shift)
        for h in range(HG_HEADS):
            a_acc[h] = a_acc[h] + jnp.where(same_blk, _dot_nt(head(q_lv, h), head(k_lv, h)), 0.0)
        hlev = blk

    d_end = jnp.exp(g_end)
    for h in range(HG_HEADS):
        o_h = _dot(a_acc[h].astype(BF16), head(vb, h))
        vt = head(v, h).T.astype(BF16)
        kd = head(k_dec, h)
        inter = []
        for j in range(nseg):
            st = st_ref[j, h]
            inter.append(_dot_nt(head(q_dec, h)[j * seg:(j + 1) * seg, :], st.astype(BF16)))
            kd_j = kd if nseg == 1 else jnp.where((rowi[:, :HG_DK] >> seg_shift) == j, kd, jnp.zeros_like(kd))
            dj = head(d_end, h)[j * seg:j * seg + 1, :]
            st_ref[j, h] = st * dj + _dot(vt, kd_j)
        o_h = o_h + (inter[0] if nseg == 1 else jnp.concatenate(inter, axis=0))
        o_ref[:, h * HG_DV:(h + 1) * HG_DV] = _rms(o_h, head(ng_ref[...], h))

    @pl.when(t == pl.num_programs(1) - 1)
    def _():
        for j in range(nseg):
            for h in range(HG_HEADS):
                so_ref[j, h] = st_ref[j, h].T


def _hgrn(qq, kk, lf, vv, s0, layer, norm_g, groups, steps, seg):
    n = qq.shape[0]
    nseg = HG_TILE // seg
    tile = pl.BlockSpec((HG_TILE, W_BR), lambda gi, ti: (gi * steps + ti, 0))
    return pl.pallas_call(
        functools.partial(_hgrn_kernel, seg=seg),
        grid=(groups, steps),
        in_specs=[tile, tile, tile, tile,
                  pl.BlockSpec((1, nseg, HG_HEADS, HG_DK, HG_DV), lambda gi, ti: (layer, gi, 0, 0, 0)),
                  pl.BlockSpec((1, W_BR), lambda gi, ti: (0, 0))],
        out_specs=[tile, pl.BlockSpec((nseg, HG_HEADS, HG_DK, HG_DV), lambda gi, ti: (gi, 0, 0, 0))],
        out_shape=[jax.ShapeDtypeStruct((n, W_BR), F32),
                   jax.ShapeDtypeStruct((groups * nseg, HG_HEADS, HG_DK, HG_DV), F32)],
        scratch_shapes=[pltpu.VMEM((nseg, HG_HEADS, HG_DV, HG_DK), F32)],
        compiler_params=_params(("parallel", "arbitrary")),
        name="hgrn",
    )(qq, kk, lf, vv, s0, norm_g)


def _merge_kernel(x_ref, ya_ref, yb_ref, yc_ref, g_ref, w4_ref, wbr_ref, bg_ref, wo_ref, fg_ref, o_ref, *, final):
    x = x_ref[...]
    hb = _rms(x, g_ref[...]).astype(BF16)
    gate0 = 3 * W_BR
    merged = jnp.zeros(x.shape, F32)
    for p, y_ref in enumerate((ya_ref, yb_ref, yc_ref)):
        z = _dot(hb, w4_ref[:, p * W_BR:(p + 1) * W_BR])
        ys = (y_ref[...] * (z * _sigmoid(z))).astype(BF16)
        br = _dot(ys, wbr_ref[p])
        gl = _dot(hb, w4_ref[:, gate0 + p * D_MODEL:gate0 + (p + 1) * D_MODEL]) + bg_ref[p:p + 1, :]
        merged = merged + _sigmoid(gl) * br
    out = x + _dot(merged.astype(BF16), wo_ref[...])
    if final:
        out = _rms(out, fg_ref[...])
    o_ref[...] = out


def _merge(x2d, ya, yb, yc, norm_g, w4, wbr, b_gate, wo, final_g, final, tm):
    n = x2d.shape[0]
    tm = min(tm, n)
    row = lambda i: (i, 0)
    fixed = lambda i: (0, 0)
    ysp = pl.BlockSpec((tm, W_BR), row)
    return pl.pallas_call(
        functools.partial(_merge_kernel, final=final),
        grid=(n // tm,),
        in_specs=[pl.BlockSpec((tm, D_MODEL), row), ysp, ysp, ysp, pl.BlockSpec((1, D_MODEL), fixed),
                  pl.BlockSpec(w4.shape, fixed), pl.BlockSpec(wbr.shape, lambda i: (0, 0, 0)),
                  pl.BlockSpec(b_gate.shape, fixed), pl.BlockSpec(wo.shape, fixed),
                  pl.BlockSpec((1, D_MODEL), fixed)],
        out_specs=pl.BlockSpec((tm, D_MODEL), row),
        out_shape=jax.ShapeDtypeStruct((n, D_MODEL), F32),
        compiler_params=_params(("parallel",)),
        name="merge",
    )(x2d, ya, yb, yc, norm_g, w4, wbr, b_gate, wo, final_g)


def _rope_tables(pos):
    half = ROPE_DIM // 2
    freqs = jnp.power(jnp.float32(ROPE_THETA), -jnp.arange(half, dtype=F32) / half)
    ang = pos.astype(F32)[:, None] * freqs[None, :]
    cos, sin = jnp.cos(ang), jnp.sin(ang)
    rows = pos.shape[0]
    ones = jnp.ones((rows, ATT_HD - ROPE_DIM), F32)
    zeros = jnp.zeros((rows, ATT_HD - ROPE_DIM), F32)
    zh = jnp.zeros((rows, half), F32)
    c_head = jnp.concatenate([cos, cos, ones], axis=1)
    sa_head = jnp.concatenate([-sin, zh, zeros], axis=1)
    sb_head = jnp.concatenate([zh, sin, zeros], axis=1)
    rep = LANES // ATT_HD
    return tuple(jnp.tile(x, (1, rep)) for x in (c_head, sa_head, sb_head))


def kernel(x_prompt, x_sample, cache_conv, cache_win_k, cache_win_v, state_hgrn, norm_g, w_in, conv_w, conv_b, ln_a_g, ln_a_b, hg_lb_param, hg_norm_g, w_branch, b_gate, w_out, final_norm_g):
    b, t, d = x_prompt.shape
    bs, ts, _ = x_sample.shape
    depth = w_in.shape[0]
    nrow = cache_win_k.shape[2]
    tm = 512

    lb_all = jnp.cumsum(jax.nn.softmax(hg_lb_param.astype(F32), axis=0), axis=0)
    lb_all = lb_all - lb_all[:1]

    rope_p = _rope_tables(jnp.arange(t, dtype=jnp.int32))
    rope_s = _rope_tables(PAST_LEN + jnp.arange(bs * ts, dtype=jnp.int32) % ts)
    bias = _decode_bias(nrow, ts)
    zero_state = jnp.zeros((1, b, HG_HEADS, HG_DK, HG_DV), F32)
    cache_kt = jnp.transpose(cache_win_k, (0, 1, 3, 4, 2))
    cache_vt = jnp.transpose(cache_win_v, (0, 1, 3, 4, 2))
    cache_conv_t = jnp.transpose(cache_conv, (0, 2, 1, 3))

    def head_major(a):
        a = jnp.transpose(a.reshape(bs, ts, ATT_HEADS, ATT_HD), (0, 2, 1, 3))
        return jnp.pad(a, ((0, 0), (0, 0), (0, DEC_Q - ts), (0, 0)))

    xp = x_prompt.reshape(b * t, d)
    xs = x_sample.reshape(bs * ts, d)
    conv_p, k_p, v_p, s_p, conv_s, s_s = [], [], [], [], [], []
    kv_out = None
    seg_s = 16
    for l in range(depth):
        wl = w_in[l]
        w1 = jnp.concatenate([wl[:, 0:2 * W_BR], wl[:, 3 * W_BR:6 * W_BR], wl[:, 7 * W_BR:10 * W_BR]], axis=1).astype(BF16)
        w4 = jnp.concatenate([wl[:, 2 * W_BR:3 * W_BR], wl[:, 6 * W_BR:7 * W_BR], wl[:, 10 * W_BR:11 * W_BR],
                              wl[:, 11 * W_BR:]], axis=1).astype(BF16)
        wbr = w_branch[l].astype(BF16)
        wo = w_out[l].astype(BF16)
        lb = lb_all[l]
        lbp = jnp.stack([jnp.log(lb), jnp.log1p(-lb), 1.0 - lb])
        ng = norm_g[l][None]
        cw, cb, lg, lbias = conv_w[l], conv_b[l][None], ln_a_g[l][None], ln_a_b[l][None]
        hng = hg_norm_g[l][None]

        up, qp, kp, vp, qqp, kkp, lfp, vvp = _inproj(xp, ng, w1, *rope_p, lbp, tm)
        us, qs, ks, vs, qqs, kks, lfs, vvs = _inproj(xs, ng, w1, *rope_s, lbp, tm)

        up3 = up.reshape(b, t, W_BR)
        ya_p = _conv_prompt(up3, cw, cb, lg, lbias).reshape(b * t, W_BR)
        conv_p.append(up3[:, t - (CONV_W - 1):])
        us_t = jnp.transpose(us.reshape(bs, ts, W_BR), (1, 0, 2))
        ya_st, nc = _conv_decode(cache_conv_t, l, us_t, cw, cb, lg, lbias)
        ya_s = jnp.transpose(ya_st, (1, 0, 2)).reshape(bs * ts, W_BR)
        conv_s.append(nc)

        qp3, kp3, vp3 = (a.reshape(b, t, W_BR) for a in (qp, kp, vp))
        yb_p = _attn_prompt(qp3, kp3, vp3).reshape(b * t, W_BR)
        keep = min(WIN_MAX, t)
        k_p.append(kp3[:, t - keep:].reshape(b, keep, ATT_HEADS, ATT_HD))
        v_p.append(vp3[:, t - keep:].reshape(b, keep, ATT_HEADS, ATT_HD))
        q8, kn8, vn8 = (head_major(a) for a in (qs, ks, vs))
        q3 = jnp.pad(jnp.tile(q8, (1, 1, len(DIL_PATTERNS), 1)), ((0, 0), (0, 0), (0, DEC_SROWS - 3 * DEC_Q), (0, 0)))
        of, k_out, v_out = _attn_decode(q3, kn8, vn8, cache_kt, cache_vt, bias, l, kv_out, ts)
        kv_out = (k_out, v_out)
        yb_s = jnp.transpose(of[:, :, :ts], (0, 2, 1, 3)).reshape(bs * ts, W_BR)

        yc_p, sp_new = _hgrn(qqp, kkp, lfp, vvp, zero_state, 0, hng, b, t // HG_TILE, HG_TILE)
        s_p.append(sp_new)

        def pad_seg(a):
            return jnp.pad(a.reshape(bs, ts, W_BR), ((0, 0), (0, seg_s - ts), (0, 0))).reshape(bs * seg_s, W_BR)

        yc_s16, ss_new = _hgrn(pad_seg(qqs), pad_seg(kks), pad_seg(lfs), pad_seg(vvs), state_hgrn, l, hng,
                               bs * seg_s // HG_TILE, 1, seg_s)
        yc_s = yc_s16.reshape(bs, seg_s, W_BR)[:, :ts].reshape(bs * ts, W_BR)
        s_s.append(ss_new)

        final = l == depth - 1
        fg = final_norm_g[None]
        xp = _merge(xp, ya_p, yb_p, yc_p, ng, w4, wbr, b_gate[l], wo, fg, final, tm)
        xs = _merge(xs, ya_s, yb_s, yc_s, ng, w4, wbr, b_gate[l], wo, fg, final, tm)

    return (xp.reshape(b, t, d), xs.reshape(bs, ts, d),
            jnp.stack(conv_p), jnp.stack(k_p), jnp.stack(v_p), jnp.stack(s_p),
            jnp.transpose(jnp.stack(conv_s), (0, 2, 1, 3)),
            jnp.transpose(kv_out[0], (0, 1, 4, 2, 3)), jnp.transpose(kv_out[1], (0, 1, 4, 2, 3)), jnp.stack(s_s))
```

```python
import functools

import jax
import jax.numpy as jnp
from jax import lax
from jax.experimental import pallas as pl
from jax.experimental.pallas import tpu as pltpu

F32 = jnp.float32
BF16 = jnp.bfloat16

D_MODEL = 1024
W_BR = 512
CONV_W = 31
ATT_HD = 64
ATT_HEADS = 8
ROPE_DIM = 16
ROPE_THETA = 500000.0
DIL_PATTERNS = ((128, 1), (512, 4), (2048, 16))
N_BACK = 128
WIN_MAX = 2048
PAST_LEN = 2048
HG_DK = 128
HG_DV = 128
HG_HEADS = 4
EPS = 1e-6
NEG = -1e30
LOG2E = 1.4426950408889634

LANES = 128
SUBLANES = 8
VMEM_LIMIT = 56 * 1024 * 1024


def _params(sem):
    return pltpu.CompilerParams(dimension_semantics=sem, vmem_limit_bytes=VMEM_LIMIT)


def _sigmoid(x):
    return 1.0 / (1.0 + jnp.exp(-x))


def _dot(a, b):
    return jnp.dot(a, b, preferred_element_type=F32)


def _dot_nt(a, b):
    return lax.dot_general(a, b, (((1,), (1,)), ((), ())), preferred_element_type=F32)


def _rms(x, g):
    return x * lax.rsqrt(jnp.mean(x * x, axis=-1, keepdims=True) + EPS) * g


def _inproj_kernel(x_ref, g_ref, w_ref, cos_ref, sa_ref, sb_ref, lbp_ref,
                   u_ref, q_ref, k_ref, v_ref, qq_ref, kk_ref, lf_ref, vv_ref):
    hb = _rms(x_ref[...], g_ref[...]).astype(BF16)

    def proj(c):
        return _dot(hb, w_ref[:, c * W_BR:(c + 1) * W_BR])

    u_ref[...] = proj(0) * _sigmoid(proj(1))

    cos = cos_ref[...]
    sa = sa_ref[...]
    sb = sb_ref[...]

    def rope_to(dst_ref, y):
        for gi in range(W_BR // LANES):
            yg = y[:, gi * LANES:(gi + 1) * LANES]
            dst_ref[:, gi * LANES:(gi + 1) * LANES] = (
                yg * cos + pltpu.roll(yg, LANES - ROPE_DIM // 2, 1) * sa + pltpu.roll(yg, ROPE_DIM // 2, 1) * sb)

    rope_to(q_ref, proj(2))
    rope_to(k_ref, proj(3))
    v_ref[...] = proj(4)

    cq = proj(5)
    qq_ref[...] = cq * _sigmoid(cq) * (HG_DK ** -0.5)

    fp = proj(6)
    log_lb = lbp_ref[0:1, :]
    log_1m_lb = lbp_ref[1:2, :]
    one_m_lb = lbp_ref[2:3, :]
    log_sig = jnp.minimum(fp, 0.0) - jnp.log1p(jnp.exp(-jnp.abs(fp)))
    b = log_1m_lb + log_sig
    lf_ref[...] = jnp.maximum(log_lb, b) + jnp.log1p(jnp.exp(-jnp.abs(log_lb - b)))
    kk_ref[...] = one_m_lb * _sigmoid(-fp)
    vv_ref[...] = proj(7)


def _inproj(x2d, norm_g, w1, cos, sa, sb, lbp, tm):
    n = x2d.shape[0]
    tm = min(tm, n)
    nrope = cos.shape[0] // tm
    row = lambda i: (i, 0)
    fixed = lambda i: (0, 0)
    rope_spec = pl.BlockSpec((tm, LANES), lambda i: (i % nrope, 0))
    return pl.pallas_call(
        _inproj_kernel,
        grid=(n // tm,),
        in_specs=[pl.BlockSpec((tm, D_MODEL), row), pl.BlockSpec((1, D_MODEL), fixed),
                  pl.BlockSpec(w1.shape, fixed), rope_spec, rope_spec, rope_spec,
                  pl.BlockSpec(lbp.shape, fixed)],
        out_specs=[pl.BlockSpec((tm, W_BR), row)] * 8,
        out_shape=[jax.ShapeDtypeStruct((n, W_BR), F32)] * 8,
        compiler_params=_params(("parallel",)),
        name="inproj",
    )(x2d, norm_g, w1, cos, sa, sb, lbp)


CONV_HALO = 32


def _ln_swish(y, lg, lb):
    mu = jnp.mean(y, axis=-1, keepdims=True)
    d = y - mu
    var = jnp.mean(d * d, axis=-1, keepdims=True)
    yn = d * lax.rsqrt(var + EPS) * lg + lb
    return yn * _sigmoid(yn)


CONV_PITCH = 2
CONV_NORM_ROWS = 128


def _conv_prompt_kernel(cur_ref, halo_ref, w_ref, b_ref, lg_ref, lb_ref, o_ref, ext_ref, cv_ref, *, tt, rc):
    i = pl.program_id(1)
    ngrp = W_BR // LANES
    nblk = rc // SUBLANES
    halo = jnp.where(i > 0, halo_ref[0], 0.0)
    off = CONV_HALO - (CONV_W - 1)
    for g in range(ngrp):
        lanes = slice(g * LANES, (g + 1) * LANES)
        ext_ref[g, pl.ds(0, CONV_HALO, stride=CONV_PITCH), :] = halo[:, lanes]
        ext_ref[g, pl.ds(CONV_PITCH * CONV_HALO, tt, stride=CONV_PITCH), :] = cur_ref[0, :, lanes]
        taps = [jnp.broadcast_to(w_ref[j:j + 1, lanes], (SUBLANES, LANES)) for j in range(CONV_W)]

        def body(c, carry, g=g, lanes=lanes, taps=taps):
            r0 = pl.multiple_of(c * rc, rc)
            acc = [None] * nblk
            for s in range(off, off + rc - SUBLANES + CONV_W):
                win = ext_ref[g, pl.ds(CONV_PITCH * (r0 + s), SUBLANES, stride=CONV_PITCH), :]
                for b in range(nblk):
                    j = s - off - SUBLANES * b
                    if 0 <= j < CONV_W:
                        term = taps[j] * win
                        acc[b] = term if acc[b] is None else acc[b] + term
            cv_ref[pl.ds(r0, rc), lanes] = jnp.concatenate(acc, axis=0)
            return carry

        lax.fori_loop(0, tt // rc, body, 0)

    def norm(c, carry):
        r0 = pl.multiple_of(c * CONV_NORM_ROWS, CONV_NORM_ROWS)
        y = cv_ref[pl.ds(r0, CONV_NORM_ROWS), :] + b_ref[...]
        o_ref[0, pl.ds(r0, CONV_NORM_ROWS), :] = _ln_swish(y, lg_ref[...], lb_ref[...])
        return carry

    lax.fori_loop(0, tt // CONV_NORM_ROWS, norm, 0)


def _conv_prompt(u, conv_w, conv_b, ln_g, ln_b, tt=512, rc=64):
    b, t, _ = u.shape
    per_tile = tt // CONV_HALO
    fixed = lambda bi, i: (0, 0)
    return pl.pallas_call(
        functools.partial(_conv_prompt_kernel, tt=tt, rc=rc),
        grid=(b, t // tt),
        in_specs=[pl.BlockSpec((1, tt, W_BR), lambda bi, i: (bi, i, 0)),
                  pl.BlockSpec((1, CONV_HALO, W_BR), lambda bi, i: (bi, jnp.maximum(i * per_tile - 1, 0), 0)),
                  pl.BlockSpec((CONV_W, W_BR), fixed), pl.BlockSpec((1, W_BR), fixed),
                  pl.BlockSpec((1, W_BR), fixed), pl.BlockSpec((1, W_BR), fixed)],
        out_specs=pl.BlockSpec((1, tt, W_BR), lambda bi, i: (bi, i, 0)),
        out_shape=jax.ShapeDtypeStruct((b, t, W_BR), F32),
        scratch_shapes=[pltpu.VMEM((W_BR // LANES, CONV_PITCH * (CONV_HALO + tt), LANES), F32),
                        pltpu.VMEM((tt, W_BR), F32)],
        compiler_params=_params(("parallel", "parallel")),
        name="conv_prompt",
    )(u, u, conv_w, conv_b, ln_g, ln_b)


def _conv_decode_kernel(cache_ref, u_ref, w_ref, b_ref, lg_ref, lb_ref, y_ref, nc_ref, *, ts):
    hist = CONV_W - 1

    def slab(r):
        return cache_ref[0, r] if r < hist else u_ref[r - hist]

    for t in range(ts):
        acc = w_ref[0:1, :] * slab(t)
        for j in range(1, CONV_W):
            acc = acc + w_ref[j:j + 1, :] * slab(t + j)
        y_ref[t] = _ln_swish(acc + b_ref[...], lg_ref[...], lb_ref[...])
    for r in range(hist):
        nc_ref[r] = slab(r + ts)


def _conv_decode(cache_t, layer, u_t, conv_w, conv_b, ln_g, ln_b, bb=32):
    ts, bs, _ = u_t.shape
    bb = min(bb, bs)
    hist = CONV_W - 1
    fixed = lambda i: (0, 0)
    return pl.pallas_call(
        functools.partial(_conv_decode_kernel, ts=ts),
        grid=(bs // bb,),
        in_specs=[pl.BlockSpec((1, hist, bb, W_BR), lambda i: (layer, 0, i, 0)),
                  pl.BlockSpec((ts, bb, W_BR), lambda i: (0, i, 0)),
                  pl.BlockSpec((CONV_W, W_BR), fixed), pl.BlockSpec((1, W_BR), fixed),
                  pl.BlockSpec((1, W_BR), fixed), pl.BlockSpec((1, W_BR), fixed)],
        out_specs=[pl.BlockSpec((ts, bb, W_BR), lambda i: (0, i, 0)),
                   pl.BlockSpec((hist, bb, W_BR), lambda i: (0, i, 0))],
        out_shape=[jax.ShapeDtypeStruct((ts, bs, W_BR), F32),
                   jax.ShapeDtypeStruct((hist, bs, W_BR), F32)],
        compiler_params=_params(("parallel",)),
        name="conv_decode",
    )(cache_t, u_t, conv_w, conv_b, ln_g, ln_b)


ATT_BLK = 128
ATT_UNROLL = 8
ATT_SKEW_SM = 2
ATT_SKEW_OUT = 4


def _attn_prompt_kernel(q_ref, k_ref, v_ref, o_ref, os_ref, ls_ref, *, t, rows_c):
    r = lax.broadcasted_iota(jnp.int32, (ATT_BLK, ATT_BLK), 0)
    c = lax.broadcasted_iota(jnp.int32, (ATT_BLK, ATT_BLK), 1)
    own_ok = c <= r
    prev_ok = c >= r
    lane = lax.broadcasted_iota(jnp.int32, (ATT_BLK, LANES), 1)
    first_head = lane < ATT_HD
    scale = ATT_HD ** -0.5

    for p, (_, dil) in enumerate(DIL_PATTERNS):
        nb = t // dil // ATT_BLK
        span = dil * ATT_BLK

        def rows(start, dil=dil):
            if dil == 1:
                return pl.ds(pl.multiple_of(start, ATT_BLK), ATT_BLK)
            return pl.ds(start, ATT_BLK, stride=dil)

        unroll = min(ATT_UNROLL, nb)
        per_cls = nb // unroll
        ncls = max(1, ATT_UNROLL // nb)

        def body(it, carry, p=p, span=span, rows=rows, unroll=unroll, per_cls=per_cls, ncls=ncls):
            starts, kb, vb = {}, {}, {}
            cls0 = it // per_cls
            first = it - cls0 * per_cls
            has_prev = first > 0
            for g in range(ncls):
                start0 = cls0 * ncls + g + span * unroll * first
                starts[g, 0] = jnp.where(has_prev, start0 - span, start0)
                for u in range(unroll):
                    starts[g, u + 1] = start0 + span * u
                for u in range(unroll + 1):
                    kb[g, u] = k_ref[0, rows(starts[g, u]), :].astype(BF16)
                    vb[g, u] = v_ref[0, rows(starts[g, u]), :].astype(BF16)
            units = [(g, u, hh) for g in range(ncls) for u in range(unroll) for hh in range(2)]
            st = {}

            def scores(j):
                g, u, hh = units[j]
                if hh == 0:
                    st["q", g, u] = q_ref[0, rows(starts[g, u + 1]), :] * scale
                head = first_head if hh == 0 else jnp.logical_not(first_head)
                qh = jnp.where(head, st["q", g, u], 0.0).astype(BF16)
                pmask = jnp.logical_and(prev_ok, has_prev) if u == 0 else prev_ok
                st["so", j] = jnp.where(own_ok, _dot_nt(qh, kb[g, u + 1]), NEG)
                st["sp", j] = jnp.where(pmask, _dot_nt(qh, kb[g, u]), NEG)

            def softmax(j):
                so, sp = st.pop(("so", j)), st.pop(("sp", j))
                m = jnp.max(jnp.maximum(so, sp), axis=-1, keepdims=True)
                po = jnp.exp(so - m)
                pp = jnp.exp(sp - m)
                l = jnp.sum(po + pp, axis=-1, keepdims=True)
                st["p", j] = (po.astype(BF16), pp.astype(BF16))
                st["inv", j] = 1.0 / l
                st["lse", j] = m + jnp.log(l)

            def output(j):
                g, u, hh = units[j]
                po, pp = st.pop(("p", j))
                st["o", j] = (_dot(po, vb[g, u + 1]) + _dot(pp, vb[g, u])) * st.pop(("inv", j))
                if hh == 1:
                    dst = rows(starts[g, u + 1])
                    os_ref[p, dst, :] = jnp.where(first_head, st.pop(("o", j - 1)), st.pop(("o", j)))
                    ls_ref[p, dst, :] = jnp.where(first_head, st.pop(("lse", j - 1)), st.pop(("lse", j)))

            nu = len(units)
            for j in range(nu + ATT_SKEW_OUT):
                if j < nu:
                    scores(j)
                if 0 <= j - ATT_SKEW_SM < nu:
                    softmax(j - ATT_SKEW_SM)
                if 0 <= j - ATT_SKEW_OUT < nu:
                    output(j - ATT_SKEW_OUT)
            return carry

        lax.fori_loop(0, dil // ncls * per_cls, body, 0)

    def combine(i, carry):
        r0 = pl.multiple_of(i * rows_c, rows_c)
        ls = [ls_ref[p, pl.ds(r0, rows_c), :] for p in range(3)]
        mx = jnp.maximum(jnp.maximum(ls[0], ls[1]), ls[2])
        ws = [jnp.exp(x - mx) for x in ls]
        num = ws[0] * os_ref[0, pl.ds(r0, rows_c), :]
        num = num + ws[1] * os_ref[1, pl.ds(r0, rows_c), :]
        num = num + ws[2] * os_ref[2, pl.ds(r0, rows_c), :]
        o_ref[0, pl.ds(r0, rows_c), :] = num / (ws[0] + ws[1] + ws[2])
        return carry

    lax.fori_loop(0, t // rows_c, combine, 0)


def _attn_prompt(q, k, v, rows_c=256):
    b, t, _ = q.shape
    spec = pl.BlockSpec((1, t, LANES), lambda bi, hp: (bi, 0, hp))
    return pl.pallas_call(
        functools.partial(_attn_prompt_kernel, t=t, rows_c=rows_c),
        grid=(b, W_BR // LANES),
        in_specs=[spec, spec, spec],
        out_specs=spec,
        out_shape=jax.ShapeDtypeStruct((b, t, W_BR), F32),
        scratch_shapes=[pltpu.VMEM((3, t, LANES), F32), pltpu.VMEM((3, t, LANES), F32)],
        compiler_params=_params(("parallel", "parallel")),
        name="attn_prompt",
    )(q, k, v)


DEC_Q = 8
DEC_SROWS = 32
DEC_HB = 8


def _attn_decode_kernel(q_ref, kn_ref, vn_ref, kc_ref, vc_ref, bias_ref, *rest, ts, aliased):
    if aliased:
        rest = rest[2:]
    o_ref, ko_ref, vo_ref = rest
    nrow = kc_ref.shape[-1]
    ext = nrow + LANES
    npat = len(DIL_PATTERNS)
    zl = jnp.zeros((DEC_Q, LANES - ATT_HD), F32)
    zr = jnp.zeros((LANES - DEC_Q, LANES), F32)
    bias = bias_ref[...]

    def new_t(x):
        x = jnp.concatenate([jnp.concatenate([x, zl], axis=1), zr], axis=0)
        return x.T[0:ATT_HD, :]

    for hh in range(DEC_HB):
        kext = jnp.concatenate([kc_ref[0, 0, hh], new_t(kn_ref[0, hh])], axis=1)
        vext = jnp.concatenate([vc_ref[0, 0, hh], new_t(vn_ref[0, hh])], axis=1)
        qh = (q_ref[0, hh] * (ATT_HD ** -0.5)).astype(BF16)
        s = _dot(qh, kext.astype(BF16)) + bias
        m = jnp.max(s, axis=-1, keepdims=True)
        pexp = jnp.exp(s - m)
        l = jnp.sum(pexp, axis=-1, keepdims=True)
        acc = _dot_nt(pexp.astype(BF16), vext.astype(BF16))
        mx = jnp.maximum(jnp.maximum(m[0:DEC_Q], m[DEC_Q:2 * DEC_Q]), m[2 * DEC_Q:3 * DEC_Q])
        num = jnp.zeros((DEC_Q, ATT_HD), F32)
        den = jnp.zeros((DEC_Q, 1), F32)
        for p in range(npat):
            w = jnp.exp(m[p * DEC_Q:(p + 1) * DEC_Q] - mx)
            num = num + w * acc[p * DEC_Q:(p + 1) * DEC_Q]
            den = den + w * l[p * DEC_Q:(p + 1) * DEC_Q]
        o_ref[0, hh] = num / den
        ko_ref[0, 0, hh] = pltpu.roll(kext, ext - ts, 1)[:, 0:nrow]
        vo_ref[0, 0, hh] = pltpu.roll(vext, ext - ts, 1)[:, 0:nrow]


def _attn_decode(q3, kn, vn, cache_kt, cache_vt, bias, layer, prev_out, ts):
    bs = q3.shape[0]
    nrow = cache_kt.shape[-1]
    cache = pl.BlockSpec((1, 1, DEC_HB, ATT_HD, nrow), lambda b, g: (layer, b, g, 0, 0))
    new = pl.BlockSpec((1, DEC_HB, DEC_Q, ATT_HD), lambda b, g: (b, g, 0, 0))
    in_specs = [pl.BlockSpec((1, DEC_HB, DEC_SROWS, ATT_HD), lambda b, g: (b, g, 0, 0)), new, new, cache, cache,
                pl.BlockSpec(bias.shape, lambda b, g: (0, 0))]
    args = [q3, kn, vn, cache_kt, cache_vt, bias]
    aliases = {}
    if prev_out is not None:
        in_specs += [pl.BlockSpec(memory_space=pl.ANY)] * 2
        aliases = {len(args): 1, len(args) + 1: 2}
        args += list(prev_out)
    return pl.pallas_call(
        functools.partial(_attn_decode_kernel, ts=ts, aliased=prev_out is not None),
        grid=(bs, ATT_HEADS // DEC_HB),
        in_specs=in_specs,
        out_specs=[new, cache, cache],
        out_shape=[jax.ShapeDtypeStruct((bs, ATT_HEADS, DEC_Q, ATT_HD), F32),
                   jax.ShapeDtypeStruct(cache_kt.shape, F32), jax.ShapeDtypeStruct(cache_vt.shape, F32)],
        input_output_aliases=aliases,
        compiler_params=_params(("parallel", "parallel")),
        name="attn_decode",
    )(*args)


def _decode_bias(nrow, ts):
    ri = jnp.arange(DEC_SROWS)
    pat, qt = ri // DEC_Q, ri % DEC_Q
    wins = jnp.array([w for w, _ in DIL_PATTERNS] + [0], jnp.int32)[pat]
    dils = jnp.array([d for _, d in DIL_PATTERNS] + [1], jnp.int32)[pat]
    real_row = (pat < len(DIL_PATTERNS)) & (qt < ts)
    ci = jnp.arange(nrow + LANES)
    key_real = ci < nrow + ts
    delta = (nrow + qt)[:, None] - ci[None, :]
    ok = (delta >= 0) & (delta % dils[:, None] == 0) & (delta <= wins[:, None]) & key_real[None, :]
    ok = ok | ~real_row[:, None]
    return jnp.where(ok, 0.0, NEG).astype(F32)


HG_TILE = 128


def _hgrn_kernel(q_ref, k_ref, g_ref, v_ref, s0_ref, ng_ref, o_ref, so_ref, st_ref, *, seg):
    c = HG_TILE
    nseg = c // seg
    t = pl.program_id(1)

    @pl.when(t == 0)
    def _():
        for j in range(nseg):
            for h in range(HG_HEADS):
                st_ref[j, h] = s0_ref[0, j, h].T

    q = q_ref[...]
    k = k_ref[...]
    g = g_ref[...]
    v = v_ref[...]
    row = lax.broadcasted_iota(jnp.int32, (c, c), 0)
    col = lax.broadcasted_iota(jnp.int32, (c, c), 1)
    rowi = lax.broadcasted_iota(jnp.int32, (c, W_BR), 0)
    seg_shift = seg.bit_length() - 1
    same_seg = (row >> seg_shift) == (col >> seg_shift)

    tri = jnp.where(jnp.logical_and(row >= col, same_seg), 1.0, 0.0).astype(BF16)
    g_hi = g.astype(BF16)
    r1 = g - g_hi.astype(F32)
    g_mid = r1.astype(BF16)
    g_lo = (r1 - g_mid.astype(F32)).astype(BF16)
    gc = (_dot(tri, g_hi) + _dot(tri, g_mid) + _dot(tri, g_lo)) * LOG2E

    def seg_last(x):
        x3 = x.reshape(nseg, seg, W_BR)
        return jnp.broadcast_to(x3[:, seg - 1:seg, :], x3.shape).reshape(c, W_BR)

    g_end = seg_last(gc)
    q_dec = (q * jnp.exp2(gc)).astype(BF16)
    k_dec = (k * jnp.exp2(g_end - gc)).astype(BF16)
    vb = v.astype(BF16)

    def head(x, h):
        return x[:, h * HG_DK:(h + 1) * HG_DK]

    qb = q.astype(BF16)
    kb = k.astype(BF16)
    a_acc = [jnp.where(row == col, _dot_nt(head(qb, h), head(kb, h)), 0.0) for h in range(HG_HEADS)]

    hlev = 1
    while hlev < seg:
        blk = 2 * hlev
        if hlev >= SUBLANES:
            nblk = c // blk
            g3 = gc.reshape(nblk, blk, W_BR)
            ref = g3[:, hlev - 1:hlev, :]
            k_lo = k.reshape(nblk, blk, W_BR)[:, 0:hlev] * jnp.exp2(ref - g3[:, 0:hlev])
            q_up = q.reshape(nblk, blk, W_BR)[:, hlev:blk] * jnp.exp2(g3[:, hlev:blk] - ref)
            zero = jnp.zeros((nblk, hlev, W_BR), F32)
            q_lv = jnp.concatenate([zero, q_up], axis=1).reshape(c, W_BR).astype(BF16)
            k_lv = jnp.concatenate([k_lo, zero], axis=1).reshape(c, W_BR).astype(BF16)
        else:
            if hlev == 4:
                g3 = gc.reshape(c // blk, blk, W_BR)
                ref = jnp.broadcast_to(g3[:, hlev - 1:hlev, :], g3.shape).reshape(c, W_BR)
            elif hlev == 1:
                ref = jnp.where((rowi & 1) == 1, pltpu.roll(gc, 1, 0), gc)
            else:
                pos = rowi & 3
                ref = jnp.where(pos == 0, pltpu.roll(gc, c - 1, 0),
                                jnp.where(pos == 1, gc,
                                          jnp.where(pos == 2, pltpu.roll(gc, 1, 0), pltpu.roll(gc, 2, 0))))
            x = jnp.exp2(-jnp.abs(gc - ref))
            upper = (rowi & (blk - 1)) >= hlev
            q_lv = jnp.where(upper, q * x, 0.0).astype(BF16)
            k_lv = jnp.where(upper, 0.0, k * x).astype(BF16)
        shift = blk.bit_length() - 1
        same_blk = (row >> shift) == (col >> shift)
        for h in range(HG_HEADS):
            a_acc[h] = a_acc[h] + jnp.where(same_blk, _dot_nt(head(q_lv, h), head(k_lv, h)), 0.0)
        hlev = blk

    d_end = jnp.exp2(g_end)
    for h in range(HG_HEADS):
        o_h = _dot(a_acc[h].astype(BF16), head(vb, h))
        vt = head(v, h).T.astype(BF16)
        kd = head(k_dec, h)
        inter = []
        for j in range(nseg):
            st = st_ref[j, h]
            inter.append(_dot_nt(head(q_dec, h)[j * seg:(j + 1) * seg, :], st.astype(BF16)))
            kd_j = kd if nseg == 1 else jnp.where((rowi[:, :HG_DK] >> seg_shift) == j, kd, jnp.zeros_like(kd))
            dj = head(d_end, h)[j * seg:j * seg + 1, :]
            st_ref[j, h] = st * dj + _dot(vt, kd_j)
        o_h = o_h + (inter[0] if nseg == 1 else jnp.concatenate(inter, axis=0))
        o_ref[:, h * HG_DV:(h + 1) * HG_DV] = _rms(o_h, head(ng_ref[...], h))

    @pl.when(t == pl.num_programs(1) - 1)
    def _():
        for j in range(nseg):
            for h in range(HG_HEADS):
                so_ref[j, h] = st_ref[j, h].T


def _hgrn(qq, kk, lf, vv, s0, layer, norm_g, groups, steps, seg):
    n = qq.shape[0]
    nseg = HG_TILE // seg
    tile = pl.BlockSpec((HG_TILE, W_BR), lambda gi, ti: (gi * steps + ti, 0))
    return pl.pallas_call(
        functools.partial(_hgrn_kernel, seg=seg),
        grid=(groups, steps),
        in_specs=[tile, tile, tile, tile,
                  pl.BlockSpec((1, nseg, HG_HEADS, HG_DK, HG_DV), lambda gi, ti: (layer, gi, 0, 0, 0)),
                  pl.BlockSpec((1, W_BR), lambda gi, ti: (0, 0))],
        out_specs=[tile, pl.BlockSpec((nseg, HG_HEADS, HG_DK, HG_DV), lambda gi, ti: (gi, 0, 0, 0))],
        out_shape=[jax.ShapeDtypeStruct((n, W_BR), F32),
                   jax.ShapeDtypeStruct((groups * nseg, HG_HEADS, HG_DK, HG_DV), F32)],
        scratch_shapes=[pltpu.VMEM((nseg, HG_HEADS, HG_DV, HG_DK), F32)],
        compiler_params=_params(("parallel", "arbitrary")),
        name="hgrn",
    )(qq, kk, lf, vv, s0, norm_g)


def _merge_kernel(x_ref, ya_ref, yb_ref, yc_ref, g_ref, w4_ref, wbr_ref, bg_ref, wo_ref, fg_ref, o_ref, *, final):
    x = x_ref[...]
    hb = _rms(x, g_ref[...]).astype(BF16)
    gate0 = 3 * W_BR
    merged = jnp.zeros(x.shape, F32)
    for p, y_ref in enumerate((ya_ref, yb_ref, yc_ref)):
        z = _dot(hb, w4_ref[:, p * W_BR:(p + 1) * W_BR])
        ys = (y_ref[...] * (z * _sigmoid(z))).astype(BF16)
        br = _dot(ys, wbr_ref[p])
        gl = _dot(hb, w4_ref[:, gate0 + p * D_MODEL:gate0 + (p + 1) * D_MODEL]) + bg_ref[p:p + 1, :]
        merged = merged + _sigmoid(gl) * br
    out = x + _dot(merged.astype(BF16), wo_ref[...])
    if final:
        out = _rms(out, fg_ref[...])
    o_ref[...] = out


def _merge(x2d, ya, yb, yc, norm_g, w4, wbr, b_gate, wo, final_g, final, tm):
    n = x2d.shape[0]
    tm = min(tm, n)
    row = lambda i: (i, 0)
    fixed = lambda i: (0, 0)
    ysp = pl.BlockSpec((tm, W_BR), row)
    return pl.pallas_call(
        functools.partial(_merge_kernel, final=final),
        grid=(n // tm,),
        in_specs=[pl.BlockSpec((tm, D_MODEL), row), ysp, ysp, ysp, pl.BlockSpec((1, D_MODEL), fixed),
                  pl.BlockSpec(w4.shape, fixed), pl.BlockSpec(wbr.shape, lambda i: (0, 0, 0)),
                  pl.BlockSpec(b_gate.shape, fixed), pl.BlockSpec(wo.shape, fixed),
                  pl.BlockSpec((1, D_MODEL), fixed)],
        out_specs=pl.BlockSpec((tm, D_MODEL), row),
        out_shape=jax.ShapeDtypeStruct((n, D_MODEL), F32),
        compiler_params=_params(("parallel",)),
        name="merge",
    )(x2d, ya, yb, yc, norm_g, w4, wbr, b_gate, wo, final_g)


def _rope_tables(pos):
    half = ROPE_DIM // 2
    freqs = jnp.power(jnp.float32(ROPE_THETA), -jnp.arange(half, dtype=F32) / half)
    ang = pos.astype(F32)[:, None] * freqs[None, :]
    cos, sin = jnp.cos(ang), jnp.sin(ang)
    rows = pos.shape[0]
    ones = jnp.ones((rows, ATT_HD - ROPE_DIM), F32)
    zeros = jnp.zeros((rows, ATT_HD - ROPE_DIM), F32)
    zh = jnp.zeros((rows, half), F32)
    c_head = jnp.concatenate([cos, cos, ones], axis=1)
    sa_head = jnp.concatenate([-sin, zh, zeros], axis=1)
    sb_head = jnp.concatenate([zh, sin, zeros], axis=1)
    rep = LANES // ATT_HD
    return tuple(jnp.tile(x, (1, rep)) for x in (c_head, sa_head, sb_head))


def kernel(x_prompt, x_sample, cache_conv, cache_win_k, cache_win_v, state_hgrn, norm_g, w_in, conv_w, conv_b, ln_a_g, ln_a_b, hg_lb_param, hg_norm_g, w_branch, b_gate, w_out, final_norm_g):
    b, t, d = x_prompt.shape
    bs, ts, _ = x_sample.shape
    depth = w_in.shape[0]
    nrow = cache_win_k.shape[2]
    tm = 512

    lb_all = jnp.cumsum(jax.nn.softmax(hg_lb_param.astype(F32), axis=0), axis=0)
    lb_all = lb_all - lb_all[:1]

    rope_p = _rope_tables(jnp.arange(t, dtype=jnp.int32))
    rope_s = _rope_tables(PAST_LEN + jnp.arange(bs * ts, dtype=jnp.int32) % ts)
    bias = _decode_bias(nrow, ts)
    zero_state = jnp.zeros((1, b, HG_HEADS, HG_DK, HG_DV), F32)
    cache_kt = jnp.transpose(cache_win_k, (0, 1, 3, 4, 2))
    cache_vt = jnp.transpose(cache_win_v, (0, 1, 3, 4, 2))
    cache_conv_t = jnp.transpose(cache_conv, (0, 2, 1, 3))

    def head_major(a):
        a = jnp.transpose(a.reshape(bs, ts, ATT_HEADS, ATT_HD), (0, 2, 1, 3))
        return jnp.pad(a, ((0, 0), (0, 0), (0, DEC_Q - ts), (0, 0)))

    xp = x_prompt.reshape(b * t, d)
    xs = x_sample.reshape(bs * ts, d)
    conv_p, k_p, v_p, s_p, conv_s, s_s = [], [], [], [], [], []
    kv_out = None
    seg_s = 16
    for l in range(depth):
        wl = w_in[l]
        w1 = jnp.concatenate([wl[:, 0:2 * W_BR], wl[:, 3 * W_BR:6 * W_BR], wl[:, 7 * W_BR:10 * W_BR]], axis=1).astype(BF16)
        w4 = jnp.concatenate([wl[:, 2 * W_BR:3 * W_BR], wl[:, 6 * W_BR:7 * W_BR], wl[:, 10 * W_BR:11 * W_BR],
                              wl[:, 11 * W_BR:]], axis=1).astype(BF16)
        wbr = w_branch[l].astype(BF16)
        wo = w_out[l].astype(BF16)
        lb = lb_all[l]
        lbp = jnp.stack([jnp.log(lb), jnp.log1p(-lb), 1.0 - lb])
        ng = norm_g[l][None]
        cw, cb, lg, lbias = conv_w[l], conv_b[l][None], ln_a_g[l][None], ln_a_b[l][None]
        hng = hg_norm_g[l][None]

        up, qp, kp, vp, qqp, kkp, lfp, vvp = _inproj(xp, ng, w1, *rope_p, lbp, tm)
        us, qs, ks, vs, qqs, kks, lfs, vvs = _inproj(xs, ng, w1, *rope_s, lbp, tm)

        up3 = up.reshape(b, t, W_BR)
        ya_p = _conv_prompt(up3, cw, cb, lg, lbias).reshape(b * t, W_BR)
        conv_p.append(up3[:, t - (CONV_W - 1):])
        us_t = jnp.transpose(us.reshape(bs, ts, W_BR), (1, 0, 2))
        ya_st, nc = _conv_decode(cache_conv_t, l, us_t, cw, cb, lg, lbias)
        ya_s = jnp.transpose(ya_st, (1, 0, 2)).reshape(bs * ts, W_BR)
        conv_s.append(nc)

        qp3, kp3, vp3 = (a.reshape(b, t, W_BR) for a in (qp, kp, vp))
        yb_p = _attn_prompt(qp3, kp3, vp3).reshape(b * t, W_BR)
        keep = min(WIN_MAX, t)
        k_p.append(kp3[:, t - keep:].reshape(b, keep, ATT_HEADS, ATT_HD))
        v_p.append(vp3[:, t - keep:].reshape(b, keep, ATT_HEADS, ATT_HD))
        q8, kn8, vn8 = (head_major(a) for a in (qs, ks, vs))
        q3 = jnp.pad(jnp.tile(q8, (1, 1, len(DIL_PATTERNS), 1)), ((0, 0), (0, 0), (0, DEC_SROWS - 3 * DEC_Q), (0, 0)))
        of, k_out, v_out = _attn_decode(q3, kn8, vn8, cache_kt, cache_vt, bias, l, kv_out, ts)
        kv_out = (k_out, v_out)
        yb_s = jnp.transpose(of[:, :, :ts], (0, 2, 1, 3)).reshape(bs * ts, W_BR)

        yc_p, sp_new = _hgrn(qqp, kkp, lfp, vvp, zero_state, 0, hng, b, t // HG_TILE, HG_TILE)
        s_p.append(sp_new)

        def pad_seg(a):
            return jnp.pad(a.reshape(bs, ts, W_BR), ((0, 0), (0, seg_s - ts), (0, 0))).reshape(bs * seg_s, W_BR)

        yc_s16, ss_new = _hgrn(pad_seg(qqs), pad_seg(kks), pad_seg(lfs), pad_seg(vvs), state_hgrn, l, hng,
                               bs * seg_s // HG_TILE, 1, seg_s)
        yc_s = yc_s16.reshape(bs, seg_s, W_BR)[:, :ts].reshape(bs * ts, W_BR)
        s_s.append(ss_new)

        final = l == depth - 1
        fg = final_norm_g[None]
        xp = _merge(xp, ya_p, yb_p, yc_p, ng, w4, wbr, b_gate[l], wo, fg, final, tm)
        xs = _merge(xs, ya_s, yb_s, yc_s, ng, w4, wbr, b_gate[l], wo, fg, final, tm)

    return (xp.reshape(b, t, d), xs.reshape(bs, ts, d),
            jnp.stack(conv_p), jnp.stack(k_p), jnp.stack(v_p), jnp.stack(s_p),
            jnp.transpose(jnp.stack(conv_s), (0, 2, 1, 3)),
            jnp.transpose(kv_out[0], (0, 1, 4, 2, 3)), jnp.transpose(kv_out[1], (0, 1, 4, 2, 3)), jnp.stack(s_s))
```

```python
import functools

import jax
import jax.numpy as jnp
from jax import lax
from jax.experimental import pallas as pl
from jax.experimental.pallas import tpu as pltpu

F32 = jnp.float32
BF16 = jnp.bfloat16

D_MODEL = 1024
W_BR = 512
CONV_W = 31
ATT_HD = 64
ATT_HEADS = 8
ROPE_DIM = 16
ROPE_THETA = 500000.0
DIL_PATTERNS = ((128, 1), (512, 4), (2048, 16))
N_BACK = 128
WIN_MAX = 2048
PAST_LEN = 2048
HG_DK = 128
HG_DV = 128
HG_HEADS = 4
EPS = 1e-6
NEG = -1e30
LOG2E = 1.4426950408889634

LANES = 128
SUBLANES = 8
VMEM_LIMIT = 56 * 1024 * 1024


def _params(sem):
    return pltpu.CompilerParams(dimension_semantics=sem, vmem_limit_bytes=VMEM_LIMIT)


def _sigmoid(x):
    return 1.0 / (1.0 + jnp.exp(-x))


def _dot(a, b):
    return jnp.dot(a, b, preferred_element_type=F32)


def _dot_nt(a, b):
    return lax.dot_general(a, b, (((1,), (1,)), ((), ())), preferred_element_type=F32)


def _rms(x, g):
    return x * lax.rsqrt(jnp.mean(x * x, axis=-1, keepdims=True) + EPS) * g


def _inproj_kernel(x_ref, g_ref, w_ref, cos_ref, sa_ref, sb_ref, lbp_ref,
                   u_ref, q_ref, k_ref, v_ref, qq_ref, kk_ref, lf_ref, vv_ref):
    hb = _rms(x_ref[...], g_ref[...]).astype(BF16)

    def proj(c):
        return _dot(hb, w_ref[:, c * W_BR:(c + 1) * W_BR])

    u_ref[...] = proj(0) * _sigmoid(proj(1))

    cos = cos_ref[...]
    sa = sa_ref[...]
    sb = sb_ref[...]

    def rope_to(dst_ref, y):
        for gi in range(W_BR // LANES):
            yg = y[:, gi * LANES:(gi + 1) * LANES]
            dst_ref[:, gi * LANES:(gi + 1) * LANES] = (
                yg * cos + pltpu.roll(yg, LANES - ROPE_DIM // 2, 1) * sa + pltpu.roll(yg, ROPE_DIM // 2, 1) * sb)

    rope_to(q_ref, proj(2))
    rope_to(k_ref, proj(3))
    v_ref[...] = proj(4)

    cq = proj(5)
    qq_ref[...] = cq * _sigmoid(cq) * (HG_DK ** -0.5)

    fp = proj(6)
    log_lb = lbp_ref[0:1, :]
    log_1m_lb = lbp_ref[1:2, :]
    one_m_lb = lbp_ref[2:3, :]
    log_sig = jnp.minimum(fp, 0.0) - jnp.log1p(jnp.exp(-jnp.abs(fp)))
    b = log_1m_lb + log_sig
    lf_ref[...] = jnp.maximum(log_lb, b) + jnp.log1p(jnp.exp(-jnp.abs(log_lb - b)))
    kk_ref[...] = one_m_lb * _sigmoid(-fp)
    vv_ref[...] = proj(7)


def _inproj(x2d, norm_g, w1, cos, sa, sb, lbp, tm):
    n = x2d.shape[0]
    tm = min(tm, n)
    nrope = cos.shape[0] // tm
    row = lambda i: (i, 0)
    fixed = lambda i: (0, 0)
    rope_spec = pl.BlockSpec((tm, LANES), lambda i: (i % nrope, 0))
    return pl.pallas_call(
        _inproj_kernel,
        grid=(n // tm,),
        in_specs=[pl.BlockSpec((tm, D_MODEL), row), pl.BlockSpec((1, D_MODEL), fixed),
                  pl.BlockSpec(w1.shape, fixed), rope_spec, rope_spec, rope_spec,
                  pl.BlockSpec(lbp.shape, fixed)],
        out_specs=[pl.BlockSpec((tm, W_BR), row)] * 8,
        out_shape=[jax.ShapeDtypeStruct((n, W_BR), F32)] * 8,
        compiler_params=_params(("parallel",)),
        name="inproj",
    )(x2d, norm_g, w1, cos, sa, sb, lbp)


CONV_HALO = 32


def _ln_swish(y, lg, lb):
    mu = jnp.mean(y, axis=-1, keepdims=True)
    d = y - mu
    var = jnp.mean(d * d, axis=-1, keepdims=True)
    yn = d * lax.rsqrt(var + EPS) * lg + lb
    return yn * _sigmoid(yn)


CONV_PITCH = 2
CONV_NORM_ROWS = 128


def _conv_prompt_kernel(cur_ref, halo_ref, w_ref, b_ref, lg_ref, lb_ref, o_ref, ext_ref, cv_ref, *, tt, rc):
    i = pl.program_id(1)
    ngrp = W_BR // LANES
    nblk = rc // SUBLANES
    halo = jnp.where(i > 0, halo_ref[0], 0.0)
    off = CONV_HALO - (CONV_W - 1)
    for g in range(ngrp):
        lanes = slice(g * LANES, (g + 1) * LANES)
        ext_ref[g, pl.ds(0, CONV_HALO, stride=CONV_PITCH), :] = halo[:, lanes]
        ext_ref[g, pl.ds(CONV_PITCH * CONV_HALO, tt, stride=CONV_PITCH), :] = cur_ref[0, :, lanes]
        taps = [jnp.broadcast_to(w_ref[j:j + 1, lanes], (SUBLANES, LANES)) for j in range(CONV_W)]

        def body(c, carry, g=g, lanes=lanes, taps=taps):
            r0 = pl.multiple_of(c * rc, rc)
            acc = [None] * nblk
            for s in range(off, off + rc - SUBLANES + CONV_W):
                win = ext_ref[g, pl.ds(CONV_PITCH * (r0 + s), SUBLANES, stride=CONV_PITCH), :]
                for b in range(nblk):
                    j = s - off - SUBLANES * b
                    if 0 <= j < CONV_W:
                        term = taps[j] * win
                        acc[b] = term if acc[b] is None else acc[b] + term
            cv_ref[pl.ds(r0, rc), lanes] = jnp.concatenate(acc, axis=0)
            return carry

        lax.fori_loop(0, tt // rc, body, 0)

    def norm(c, carry):
        r0 = pl.multiple_of(c * CONV_NORM_ROWS, CONV_NORM_ROWS)
        y = cv_ref[pl.ds(r0, CONV_NORM_ROWS), :] + b_ref[...]
        o_ref[0, pl.ds(r0, CONV_NORM_ROWS), :] = _ln_swish(y, lg_ref[...], lb_ref[...])
        return carry

    lax.fori_loop(0, tt // CONV_NORM_ROWS, norm, 0)


def _conv_prompt(u, conv_w, conv_b, ln_g, ln_b, tt=512, rc=64):
    b, t, _ = u.shape
    per_tile = tt // CONV_HALO
    fixed = lambda bi, i: (0, 0)
    return pl.pallas_call(
        functools.partial(_conv_prompt_kernel, tt=tt, rc=rc),
        grid=(b, t // tt),
        in_specs=[pl.BlockSpec((1, tt, W_BR), lambda bi, i: (bi, i, 0)),
                  pl.BlockSpec((1, CONV_HALO, W_BR), lambda bi, i: (bi, jnp.maximum(i * per_tile - 1, 0), 0)),
                  pl.BlockSpec((CONV_W, W_BR), fixed), pl.BlockSpec((1, W_BR), fixed),
                  pl.BlockSpec((1, W_BR), fixed), pl.BlockSpec((1, W_BR), fixed)],
        out_specs=pl.BlockSpec((1, tt, W_BR), lambda bi, i: (bi, i, 0)),
        out_shape=jax.ShapeDtypeStruct((b, t, W_BR), F32),
        scratch_shapes=[pltpu.VMEM((W_BR // LANES, CONV_PITCH * (CONV_HALO + tt), LANES), F32),
                        pltpu.VMEM((tt, W_BR), F32)],
        compiler_params=_params(("parallel", "parallel")),
        name="conv_prompt",
    )(u, u, conv_w, conv_b, ln_g, ln_b)


def _conv_decode_kernel(cache_ref, u_ref, w_ref, b_ref, lg_ref, lb_ref, y_ref, nc_ref, *, ts):
    hist = CONV_W - 1

    def slab(r):
        return cache_ref[0, r] if r < hist else u_ref[r - hist]

    for t in range(ts):
        acc = w_ref[0:1, :] * slab(t)
        for j in range(1, CONV_W):
            acc = acc + w_ref[j:j + 1, :] * slab(t + j)
        y_ref[t] = _ln_swish(acc + b_ref[...], lg_ref[...], lb_ref[...])
    for r in range(hist):
        nc_ref[r] = slab(r + ts)


def _conv_decode(cache_t, layer, u_t, conv_w, conv_b, ln_g, ln_b, bb=32):
    ts, bs, _ = u_t.shape
    bb = min(bb, bs)
    hist = CONV_W - 1
    fixed = lambda i: (0, 0)
    return pl.pallas_call(
        functools.partial(_conv_decode_kernel, ts=ts),
        grid=(bs // bb,),
        in_specs=[pl.BlockSpec((1, hist, bb, W_BR), lambda i: (layer, 0, i, 0)),
                  pl.BlockSpec((ts, bb, W_BR), lambda i: (0, i, 0)),
                  pl.BlockSpec((CONV_W, W_BR), fixed), pl.BlockSpec((1, W_BR), fixed),
                  pl.BlockSpec((1, W_BR), fixed), pl.BlockSpec((1, W_BR), fixed)],
        out_specs=[pl.BlockSpec((ts, bb, W_BR), lambda i: (0, i, 0)),
                   pl.BlockSpec((hist, bb, W_BR), lambda i: (0, i, 0))],
        out_shape=[jax.ShapeDtypeStruct((ts, bs, W_BR), F32),
                   jax.ShapeDtypeStruct((hist, bs, W_BR), F32)],
        compiler_params=_params(("parallel",)),
        name="conv_decode",
    )(cache_t, u_t, conv_w, conv_b, ln_g, ln_b)


ATT_BLK = 128
ATT_UNROLL = 8
ATT_SKEW_SM = 2
ATT_SKEW_OUT = 4


def _attn_prompt_items(t):
    return len(DIL_PATTERNS) * (t // ATT_BLK // ATT_UNROLL) + 1


def _attn_prompt_work(q_ref, k_ref, v_ref, o_ref, os_ref, ls_ref, item, *, t, rows_c):
    trips = t // ATT_BLK // ATT_UNROLL
    r = lax.broadcasted_iota(jnp.int32, (ATT_BLK, ATT_BLK), 0)
    c = lax.broadcasted_iota(jnp.int32, (ATT_BLK, ATT_BLK), 1)
    own_ok = c <= r
    prev_ok = c >= r
    lane = lax.broadcasted_iota(jnp.int32, (ATT_BLK, LANES), 1)
    first_head = lane < ATT_HD
    scale = ATT_HD ** -0.5

    for p, (_, dil) in enumerate(DIL_PATTERNS):
        nb = t // dil // ATT_BLK
        span = dil * ATT_BLK

        def rows(start, dil=dil):
            if dil == 1:
                return pl.ds(pl.multiple_of(start, ATT_BLK), ATT_BLK)
            return pl.ds(start, ATT_BLK, stride=dil)

        unroll = min(ATT_UNROLL, nb)
        per_cls = nb // unroll
        ncls = max(1, ATT_UNROLL // nb)

        assert dil // ncls * per_cls == trips

        def body(it, p=p, span=span, rows=rows, unroll=unroll, per_cls=per_cls, ncls=ncls):
            starts, kb, vb = {}, {}, {}
            cls0 = it // per_cls
            first = it - cls0 * per_cls
            has_prev = first > 0
            for g in range(ncls):
                start0 = cls0 * ncls + g + span * unroll * first
                starts[g, 0] = jnp.where(has_prev, start0 - span, start0)
                for u in range(unroll):
                    starts[g, u + 1] = start0 + span * u
                for u in range(unroll + 1):
                    kb[g, u] = k_ref[0, rows(starts[g, u]), :].astype(BF16)
                    vb[g, u] = v_ref[0, rows(starts[g, u]), :].astype(BF16)
            units = [(g, u, hh) for g in range(ncls) for u in range(unroll) for hh in range(2)]
            st = {}

            def scores(j):
                g, u, hh = units[j]
                if hh == 0:
                    st["q", g, u] = q_ref[0, rows(starts[g, u + 1]), :] * scale
                head = first_head if hh == 0 else jnp.logical_not(first_head)
                qh = jnp.where(head, st["q", g, u], 0.0).astype(BF16)
                pmask = jnp.logical_and(prev_ok, has_prev) if u == 0 else prev_ok
                st["so", j] = jnp.where(own_ok, _dot_nt(qh, kb[g, u + 1]), NEG)
                st["sp", j] = jnp.where(pmask, _dot_nt(qh, kb[g, u]), NEG)

            def softmax(j):
                so, sp = st.pop(("so", j)), st.pop(("sp", j))
                m = jnp.max(jnp.maximum(so, sp), axis=-1, keepdims=True)
                po = jnp.exp(so - m)
                pp = jnp.exp(sp - m)
                l = jnp.sum(po + pp, axis=-1, keepdims=True)
                st["p", j] = (po.astype(BF16), pp.astype(BF16))
                st["inv", j] = 1.0 / l
                st["lse", j] = m + jnp.log(l)

            def output(j):
                g, u, hh = units[j]
                po, pp = st.pop(("p", j))
                st["o", j] = (_dot(po, vb[g, u + 1]) + _dot(pp, vb[g, u])) * st.pop(("inv", j))
                if hh == 1:
                    dst = rows(starts[g, u + 1])
                    os_ref[p, dst, :] = jnp.where(first_head, st.pop(("o", j - 1)), st.pop(("o", j)))
                    ls_ref[p, dst, :] = jnp.where(first_head, st.pop(("lse", j - 1)), st.pop(("lse", j)))

            nu = len(units)
            for j in range(nu + ATT_SKEW_OUT):
                if j < nu:
                    scores(j)
                if 0 <= j - ATT_SKEW_SM < nu:
                    softmax(j - ATT_SKEW_SM)
                if 0 <= j - ATT_SKEW_OUT < nu:
                    output(j - ATT_SKEW_OUT)

        pl.when(jnp.logical_and(item >= p * trips, item < (p + 1) * trips))(
            functools.partial(body, item - p * trips))

    def combine(i, carry):
        r0 = pl.multiple_of(i * rows_c, rows_c)
        ls = [ls_ref[p, pl.ds(r0, rows_c), :] for p in range(3)]
        mx = jnp.maximum(jnp.maximum(ls[0], ls[1]), ls[2])
        ws = [jnp.exp(x - mx) for x in ls]
        num = ws[0] * os_ref[0, pl.ds(r0, rows_c), :]
        num = num + ws[1] * os_ref[1, pl.ds(r0, rows_c), :]
        num = num + ws[2] * os_ref[2, pl.ds(r0, rows_c), :]
        o_ref[0, pl.ds(r0, rows_c), :] = num / (ws[0] + ws[1] + ws[2])
        return carry

    @pl.when(item == len(DIL_PATTERNS) * trips)
    def _():
        lax.fori_loop(0, t // rows_c, combine, 0)


DEC_Q = 8
DEC_SROWS = 32
DEC_HB = 4


def _attn_decode_work(q_ref, kn_ref, vn_ref, kc_ref, vc_ref, bias_ref, o_ref, ko_ref, vo_ref, *, ts):
    nrow = kc_ref.shape[-1]
    ext = nrow + LANES
    npat = len(DIL_PATTERNS)
    zl = jnp.zeros((DEC_Q, LANES - ATT_HD), F32)
    zr = jnp.zeros((LANES - DEC_Q, LANES), F32)
    bias = bias_ref[...]

    def new_t(x):
        x = jnp.concatenate([jnp.concatenate([x, zl], axis=1), zr], axis=0)
        return x.T[0:ATT_HD, :]

    for hh in range(DEC_HB):
        kext = jnp.concatenate([kc_ref[0, 0, hh], new_t(kn_ref[0, hh])], axis=1)
        vext = jnp.concatenate([vc_ref[0, 0, hh], new_t(vn_ref[0, hh])], axis=1)
        qh = (q_ref[0, hh] * (ATT_HD ** -0.5)).astype(BF16)
        s = _dot(qh, kext.astype(BF16)) + bias
        m = jnp.max(s, axis=-1, keepdims=True)
        pexp = jnp.exp(s - m)
        l = jnp.sum(pexp, axis=-1, keepdims=True)
        acc = _dot_nt(pexp.astype(BF16), vext.astype(BF16))
        mx = jnp.maximum(jnp.maximum(m[0:DEC_Q], m[DEC_Q:2 * DEC_Q]), m[2 * DEC_Q:3 * DEC_Q])
        num = jnp.zeros((DEC_Q, ATT_HD), F32)
        den = jnp.zeros((DEC_Q, 1), F32)
        for p in range(npat):
            w = jnp.exp(m[p * DEC_Q:(p + 1) * DEC_Q] - mx)
            num = num + w * acc[p * DEC_Q:(p + 1) * DEC_Q]
            den = den + w * l[p * DEC_Q:(p + 1) * DEC_Q]
        o_ref[0, hh] = num / den
        ko_ref[0, 0, hh] = pltpu.roll(kext, ext - ts, 1)[:, 0:nrow]
        vo_ref[0, 0, hh] = pltpu.roll(vext, ext - ts, 1)[:, 0:nrow]


def _attn_kernel(q3_ref, kn_ref, vn_ref, kc_ref, vc_ref, bias_ref, qp_ref, kp_ref, vp_ref, *rest,
                 ts, t, rows_c, sub_steps, items_per_step, aliased):
    if aliased:
        rest = rest[2:]
    od_ref, ko_ref, vo_ref, op_ref, os_ref, ls_ref = rest
    _attn_decode_work(q3_ref, kn_ref, vn_ref, kc_ref, vc_ref, bias_ref, od_ref, ko_ref, vo_ref, ts=ts)
    sub = pl.program_id(0) % sub_steps
    for i in range(items_per_step):
        _attn_prompt_work(qp_ref, kp_ref, vp_ref, op_ref, os_ref, ls_ref, sub * items_per_step + i, t=t, rows_c=rows_c)


def _attn(q3, kn, vn, cache_kt, cache_vt, bias, layer, prev_out, ts, qp, kp, vp, rows_c=256):
    bs = q3.shape[0]
    nrow = cache_kt.shape[-1]
    b, t, _ = qp.shape
    ngrp = ATT_HEADS // DEC_HB
    npair = W_BR // LANES
    nsteps = bs * ngrp
    sub_steps = nsteps // (b * npair)
    assert sub_steps * b * npair == nsteps
    items_per_step = -(-_attn_prompt_items(t) // sub_steps)
    cache = pl.BlockSpec((1, 1, DEC_HB, ATT_HD, nrow), lambda i: (layer, i // ngrp, i % ngrp, 0, 0))
    new = pl.BlockSpec((1, DEC_HB, DEC_Q, ATT_HD), lambda i: (i // ngrp, i % ngrp, 0, 0))
    prompt = pl.BlockSpec((1, t, LANES), lambda i: (i // sub_steps // npair, 0, i // sub_steps % npair))
    in_specs = [pl.BlockSpec((1, DEC_HB, DEC_SROWS, ATT_HD), lambda i: (i // ngrp, i % ngrp, 0, 0)), new, new,
                cache, cache, pl.BlockSpec(bias.shape, lambda i: (0, 0)), prompt, prompt, prompt]
    args = [q3, kn, vn, cache_kt, cache_vt, bias, qp, kp, vp]
    aliases = {}
    if prev_out is not None:
        in_specs += [pl.BlockSpec(memory_space=pl.ANY)] * 2
        aliases = {len(args): 1, len(args) + 1: 2}
        args += list(prev_out)
    return pl.pallas_call(
        functools.partial(_attn_kernel, ts=ts, t=t, rows_c=rows_c, sub_steps=sub_steps,
                          items_per_step=items_per_step, aliased=prev_out is not None),
        grid=(nsteps,),
        in_specs=in_specs,
        out_specs=[new, cache, cache, prompt],
        out_shape=[jax.ShapeDtypeStruct((bs, ATT_HEADS, DEC_Q, ATT_HD), F32),
                   jax.ShapeDtypeStruct(cache_kt.shape, F32), jax.ShapeDtypeStruct(cache_vt.shape, F32),
                   jax.ShapeDtypeStruct((b, t, W_BR), F32)],
        scratch_shapes=[pltpu.VMEM((3, t, LANES), F32), pltpu.VMEM((3, t, LANES), F32)],
        input_output_aliases=aliases,
        compiler_params=_params(("arbitrary",)),
        name="attn",
    )(*args)


def _decode_bias(nrow, ts):
    ri = jnp.arange(DEC_SROWS)
    pat, qt = ri // DEC_Q, ri % DEC_Q
    wins = jnp.array([w for w, _ in DIL_PATTERNS] + [0], jnp.int32)[pat]
    dils = jnp.array([d for _, d in DIL_PATTERNS] + [1], jnp.int32)[pat]
    real_row = (pat < len(DIL_PATTERNS)) & (qt < ts)
    ci = jnp.arange(nrow + LANES)
    key_real = ci < nrow + ts
    delta = (nrow + qt)[:, None] - ci[None, :]
    ok = (delta >= 0) & (delta % dils[:, None] == 0) & (delta <= wins[:, None]) & key_real[None, :]
    ok = ok | ~real_row[:, None]
    return jnp.where(ok, 0.0, NEG).astype(F32)


HG_TILE = 128


def _hgrn_kernel(q_ref, k_ref, g_ref, v_ref, s0_ref, ng_ref, *rest, seg, aliased):
    o_ref, so_ref, st_ref = rest[1:] if aliased else rest
    c = HG_TILE
    nseg = c // seg
    t = pl.program_id(1)

    @pl.when(t == 0)
    def _():
        for j in range(nseg):
            for h in range(HG_HEADS):
                st_ref[j, h] = s0_ref[0, j, h].T

    q = q_ref[...]
    k = k_ref[...]
    g = g_ref[...]
    v = v_ref[...]
    row = lax.broadcasted_iota(jnp.int32, (c, c), 0)
    col = lax.broadcasted_iota(jnp.int32, (c, c), 1)
    rowi = lax.broadcasted_iota(jnp.int32, (c, W_BR), 0)
    seg_shift = seg.bit_length() - 1
    same_seg = (row >> seg_shift) == (col >> seg_shift)

    tri = jnp.where(jnp.logical_and(row >= col, same_seg), 1.0, 0.0).astype(BF16)
    g_hi = g.astype(BF16)
    r1 = g - g_hi.astype(F32)
    g_mid = r1.astype(BF16)
    g_lo = (r1 - g_mid.astype(F32)).astype(BF16)
    gc = (_dot(tri, g_hi) + _dot(tri, g_mid) + _dot(tri, g_lo)) * LOG2E

    def seg_last(x):
        x3 = x.reshape(nseg, seg, W_BR)
        return jnp.broadcast_to(x3[:, seg - 1:seg, :], x3.shape).reshape(c, W_BR)

    g_end = seg_last(gc)
    q_dec = (q * jnp.exp2(gc)).astype(BF16)
    k_dec = (k * jnp.exp2(g_end - gc)).astype(BF16)
    vb = v.astype(BF16)

    def head(x, h):
        return x[:, h * HG_DK:(h + 1) * HG_DK]

    qb = q.astype(BF16)
    kb = k.astype(BF16)
    a_acc = [jnp.where(row == col, _dot_nt(head(qb, h), head(kb, h)), 0.0) for h in range(HG_HEADS)]

    hlev = 1
    while hlev < seg:
        blk = 2 * hlev
        if hlev >= SUBLANES:
            nblk = c // blk
            g3 = gc.reshape(nblk, blk, W_BR)
            ref = g3[:, hlev - 1:hlev, :]
            k_lo = k.reshape(nblk, blk, W_BR)[:, 0:hlev] * jnp.exp2(ref - g3[:, 0:hlev])
            q_up = q.reshape(nblk, blk, W_BR)[:, hlev:blk] * jnp.exp2(g3[:, hlev:blk] - ref)
            zero = jnp.zeros((nblk, hlev, W_BR), F32)
            q_lv = jnp.concatenate([zero, q_up], axis=1).reshape(c, W_BR).astype(BF16)
            k_lv = jnp.concatenate([k_lo, zero], axis=1).reshape(c, W_BR).astype(BF16)
        else:
            if hlev == 4:
                g3 = gc.reshape(c // blk, blk, W_BR)
                ref = jnp.broadcast_to(g3[:, hlev - 1:hlev, :], g3.shape).reshape(c, W_BR)
            elif hlev == 1:
                ref = jnp.where((rowi & 1) == 1, pltpu.roll(gc, 1, 0), gc)
            else:
                pos = rowi & 3
                ref = jnp.where(pos == 0, pltpu.roll(gc, c - 1, 0),
                                jnp.where(pos == 1, gc,
                                          jnp.where(pos == 2, pltpu.roll(gc, 1, 0), pltpu.roll(gc, 2, 0))))
            x = jnp.exp2(-jnp.abs(gc - ref))
            upper = (rowi & (blk - 1)) >= hlev
            q_lv = jnp.where(upper, q * x, 0.0).astype(BF16)
            k_lv = jnp.where(upper, 0.0, k * x).astype(BF16)
        shift = blk.bit_length() - 1
        same_blk = (row >> shift) == (col >> shift)
        for h in range(HG_HEADS):
            a_acc[h] = a_acc[h] + jnp.where(same_blk, _dot_nt(head(q_lv, h), head(k_lv, h)), 0.0)
        hlev = blk

    d_end = jnp.exp2(g_end)
    for h in range(HG_HEADS):
        o_h = _dot(a_acc[h].astype(BF16), head(vb, h))
        vt = head(v, h).T.astype(BF16)
        kd = head(k_dec, h)
        inter = []
        for j in range(nseg):
            st = st_ref[j, h]
            inter.append(_dot_nt(head(q_dec, h)[j * seg:(j + 1) * seg, :], st.astype(BF16)))
            kd_j = kd if nseg == 1 else jnp.where((rowi[:, :HG_DK] >> seg_shift) == j, kd, jnp.zeros_like(kd))
            dj = head(d_end, h)[j * seg:j * seg + 1, :]
            st_ref[j, h] = st * dj + _dot(vt, kd_j)
        o_h = o_h + (inter[0] if nseg == 1 else jnp.concatenate(inter, axis=0))
        o_ref[:, h * HG_DV:(h + 1) * HG_DV] = _rms(o_h, head(ng_ref[...], h))

    @pl.when(t == pl.num_programs(1) - 1)
    def _():
        for j in range(nseg):
            for h in range(HG_HEADS):
                so_ref[0, j, h] = st_ref[j, h].T


def _hgrn(qq, kk, lf, vv, s0, layer_in, norm_g, groups, steps, seg, depth, layer_out, prev_states):
    n = qq.shape[0]
    nseg = HG_TILE // seg
    tile = pl.BlockSpec((HG_TILE, W_BR), lambda gi, ti: (gi * steps + ti, 0))
    in_specs = [tile, tile, tile, tile,
                pl.BlockSpec((1, nseg, HG_HEADS, HG_DK, HG_DV), lambda gi, ti: (layer_in, gi, 0, 0, 0)),
                pl.BlockSpec((1, W_BR), lambda gi, ti: (0, 0))]
    args = [qq, kk, lf, vv, s0, norm_g]
    aliases = {}
    if prev_states is not None:
        in_specs.append(pl.BlockSpec(memory_space=pl.ANY))
        aliases = {len(args): 1}
        args.append(prev_states)
    return pl.pallas_call(
        functools.partial(_hgrn_kernel, seg=seg, aliased=prev_states is not None),
        grid=(groups, steps),
        in_specs=in_specs,
        out_specs=[tile, pl.BlockSpec((1, nseg, HG_HEADS, HG_DK, HG_DV), lambda gi, ti: (layer_out, gi, 0, 0, 0))],
        out_shape=[jax.ShapeDtypeStruct((n, W_BR), F32),
                   jax.ShapeDtypeStruct((depth, groups * nseg, HG_HEADS, HG_DK, HG_DV), F32)],
        scratch_shapes=[pltpu.VMEM((nseg, HG_HEADS, HG_DV, HG_DK), F32)],
        input_output_aliases=aliases,
        compiler_params=_params(("parallel", "arbitrary")),
        name="hgrn",
    )(*args)


def _merge_kernel(x_ref, ya_ref, yb_ref, yc_ref, g_ref, w4_ref, wbr_ref, bg_ref, wo_ref, fg_ref, o_ref, *, final):
    x = x_ref[...]
    hb = _rms(x, g_ref[...]).astype(BF16)
    gate0 = 3 * W_BR
    merged = jnp.zeros(x.shape, F32)
    for p, y_ref in enumerate((ya_ref, yb_ref, yc_ref)):
        z = _dot(hb, w4_ref[:, p * W_BR:(p + 1) * W_BR])
        ys = (y_ref[...] * (z * _sigmoid(z))).astype(BF16)
        br = _dot(ys, wbr_ref[p])
        gl = _dot(hb, w4_ref[:, gate0 + p * D_MODEL:gate0 + (p + 1) * D_MODEL]) + bg_ref[p:p + 1, :]
        merged = merged + _sigmoid(gl) * br
    out = x + _dot(merged.astype(BF16), wo_ref[...])
    if final:
        out = _rms(out, fg_ref[...])
    o_ref[...] = out


def _merge(x2d, ya, yb, yc, norm_g, w4, wbr, b_gate, wo, final_g, final, tm):
    n = x2d.shape[0]
    tm = min(tm, n)
    row = lambda i: (i, 0)
    fixed = lambda i: (0, 0)
    ysp = pl.BlockSpec((tm, W_BR), row)
    return pl.pallas_call(
        functools.partial(_merge_kernel, final=final),
        grid=(n // tm,),
        in_specs=[pl.BlockSpec((tm, D_MODEL), row), ysp, ysp, ysp, pl.BlockSpec((1, D_MODEL), fixed),
                  pl.BlockSpec(w4.shape, fixed), pl.BlockSpec(wbr.shape, lambda i: (0, 0, 0)),
                  pl.BlockSpec(b_gate.shape, fixed), pl.BlockSpec(wo.shape, fixed),
                  pl.BlockSpec((1, D_MODEL), fixed)],
        out_specs=pl.BlockSpec((tm, D_MODEL), row),
        out_shape=jax.ShapeDtypeStruct((n, D_MODEL), F32),
        compiler_params=_params(("parallel",)),
        name="merge",
    )(x2d, ya, yb, yc, norm_g, w4, wbr, b_gate, wo, final_g)


def _rope_tables(pos):
    half = ROPE_DIM // 2
    freqs = jnp.power(jnp.float32(ROPE_THETA), -jnp.arange(half, dtype=F32) / half)
    ang = pos.astype(F32)[:, None] * freqs[None, :]
    cos, sin = jnp.cos(ang), jnp.sin(ang)
    rows = pos.shape[0]
    ones = jnp.ones((rows, ATT_HD - ROPE_DIM), F32)
    zeros = jnp.zeros((rows, ATT_HD - ROPE_DIM), F32)
    zh = jnp.zeros((rows, half), F32)
    c_head = jnp.concatenate([cos, cos, ones], axis=1)
    sa_head = jnp.concatenate([-sin, zh, zeros], axis=1)
    sb_head = jnp.concatenate([zh, sin, zeros], axis=1)
    rep = LANES // ATT_HD
    return tuple(jnp.tile(x, (1, rep)) for x in (c_head, sa_head, sb_head))


def kernel(x_prompt, x_sample, cache_conv, cache_win_k, cache_win_v, state_hgrn, norm_g, w_in, conv_w, conv_b, ln_a_g, ln_a_b, hg_lb_param, hg_norm_g, w_branch, b_gate, w_out, final_norm_g):
    b, t, d = x_prompt.shape
    bs, ts, _ = x_sample.shape
    depth = w_in.shape[0]
    nrow = cache_win_k.shape[2]
    tm = 512

    lb_all = jnp.cumsum(jax.nn.softmax(hg_lb_param.astype(F32), axis=0), axis=0)
    lb_all = lb_all - lb_all[:1]

    rope_p = _rope_tables(jnp.arange(t, dtype=jnp.int32))
    rope_s = _rope_tables(PAST_LEN + jnp.arange(bs * ts, dtype=jnp.int32) % ts)
    bias = _decode_bias(nrow, ts)
    zero_state = jnp.zeros((1, b, HG_HEADS, HG_DK, HG_DV), F32)
    cache_kt = jnp.transpose(cache_win_k, (0, 1, 3, 4, 2))
    cache_vt = jnp.transpose(cache_win_v, (0, 1, 3, 4, 2))
    cache_conv_t = jnp.transpose(cache_conv, (0, 2, 1, 3))

    def head_major(a):
        a = jnp.transpose(a.reshape(bs, ts, ATT_HEADS, ATT_HD), (0, 2, 1, 3))
        return jnp.pad(a, ((0, 0), (0, 0), (0, DEC_Q - ts), (0, 0)))

    xp = x_prompt.reshape(b * t, d)
    xs = x_sample.reshape(bs * ts, d)
    conv_p, k_p, v_p, conv_s = [], [], [], []
    s_p = s_s = None
    kv_out = None
    seg_s = 16
    for l in range(depth):
        wl = w_in[l]
        w1 = jnp.concatenate([wl[:, 0:2 * W_BR], wl[:, 3 * W_BR:6 * W_BR], wl[:, 7 * W_BR:10 * W_BR]], axis=1).astype(BF16)
        w4 = jnp.concatenate([wl[:, 2 * W_BR:3 * W_BR], wl[:, 6 * W_BR:7 * W_BR], wl[:, 10 * W_BR:11 * W_BR],
                              wl[:, 11 * W_BR:]], axis=1).astype(BF16)
        wbr = w_branch[l].astype(BF16)
        wo = w_out[l].astype(BF16)
        lb = lb_all[l]
        lbp = jnp.stack([jnp.log(lb), jnp.log1p(-lb), 1.0 - lb])
        ng = norm_g[l][None]
        cw, cb, lg, lbias = conv_w[l], conv_b[l][None], ln_a_g[l][None], ln_a_b[l][None]
        hng = hg_norm_g[l][None]

        up, qp, kp, vp, qqp, kkp, lfp, vvp = _inproj(xp, ng, w1, *rope_p, lbp, tm)
        us, qs, ks, vs, qqs, kks, lfs, vvs = _inproj(xs, ng, w1, *rope_s, lbp, tm)

        up3 = up.reshape(b, t, W_BR)
        ya_p = _conv_prompt(up3, cw, cb, lg, lbias).reshape(b * t, W_BR)
        conv_p.append(up3[:, t - (CONV_W - 1):])
        us_t = jnp.transpose(us.reshape(bs, ts, W_BR), (1, 0, 2))
        ya_st, nc = _conv_decode(cache_conv_t, l, us_t, cw, cb, lg, lbias)
        ya_s = jnp.transpose(ya_st, (1, 0, 2)).reshape(bs * ts, W_BR)
        conv_s.append(nc)

        qp3, kp3, vp3 = (a.reshape(b, t, W_BR) for a in (qp, kp, vp))
        keep = min(WIN_MAX, t)
        k_p.append(kp3[:, t - keep:].reshape(b, keep, ATT_HEADS, ATT_HD))
        v_p.append(vp3[:, t - keep:].reshape(b, keep, ATT_HEADS, ATT_HD))
        q8, kn8, vn8 = (head_major(a) for a in (qs, ks, vs))
        q3 = jnp.pad(jnp.tile(q8, (1, 1, len(DIL_PATTERNS), 1)), ((0, 0), (0, 0), (0, DEC_SROWS - 3 * DEC_Q), (0, 0)))
        of, k_out, v_out, yb_p3 = _attn(q3, kn8, vn8, cache_kt, cache_vt, bias, l, kv_out, ts, qp3, kp3, vp3)
        kv_out = (k_out, v_out)
        yb_p = yb_p3.reshape(b * t, W_BR)
        yb_s = jnp.transpose(of[:, :, :ts], (0, 2, 1, 3)).reshape(bs * ts, W_BR)

        yc_p, s_p = _hgrn(qqp, kkp, lfp, vvp, zero_state, 0, hng, b, t // HG_TILE, HG_TILE, depth, l, s_p)

        def pad_seg(a):
            return jnp.pad(a.reshape(bs, ts, W_BR), ((0, 0), (0, seg_s - ts), (0, 0))).reshape(bs * seg_s, W_BR)

        yc_s16, s_s = _hgrn(pad_seg(qqs), pad_seg(kks), pad_seg(lfs), pad_seg(vvs), state_hgrn, l, hng,
                            bs * seg_s // HG_TILE, 1, seg_s, depth, l, s_s)
        yc_s = yc_s16.reshape(bs, seg_s, W_BR)[:, :ts].reshape(bs * ts, W_BR)

        final = l == depth - 1
        fg = final_norm_g[None]
        xp = _merge(xp, ya_p, yb_p, yc_p, ng, w4, wbr, b_gate[l], wo, fg, final, tm)
        xs = _merge(xs, ya_s, yb_s, yc_s, ng, w4, wbr, b_gate[l], wo, fg, final, tm)

    return (xp.reshape(b, t, d), xs.reshape(bs, ts, d),
            jnp.stack(conv_p), jnp.stack(k_p), jnp.stack(v_p), s_p,
            jnp.transpose(jnp.stack(conv_s), (0, 2, 1, 3)),
            jnp.transpose(kv_out[0], (0, 1, 4, 2, 3)), jnp.transpose(kv_out[1], (0, 1, 4, 2, 3)), s_s)
```

```python
import functools

import jax
import jax.numpy as jnp
from jax import lax
from jax.experimental import pallas as pl
from jax.experimental.pallas import tpu as pltpu

F32 = jnp.float32
BF16 = jnp.bfloat16

D_MODEL = 1024
W_BR = 512
CONV_W = 31
ATT_HD = 64
ATT_HEADS = 8
ROPE_DIM = 16
ROPE_THETA = 500000.0
DIL_PATTERNS = ((128, 1), (512, 4), (2048, 16))
N_BACK = 128
WIN_MAX = 2048
PAST_LEN = 2048
HG_DK = 128
HG_DV = 128
HG_HEADS = 4
EPS = 1e-6
NEG = -1e30
LOG2E = 1.4426950408889634

LANES = 128
SUBLANES = 8
VMEM_LIMIT = 56 * 1024 * 1024


def _params(sem):
    return pltpu.CompilerParams(dimension_semantics=sem, vmem_limit_bytes=VMEM_LIMIT)


def _sigmoid(x):
    return 1.0 / (1.0 + jnp.exp(-x))


def _dot(a, b):
    return jnp.dot(a, b, preferred_element_type=F32)


def _dot_nt(a, b):
    return lax.dot_general(a, b, (((1,), (1,)), ((), ())), preferred_element_type=F32)


def _rms(x, g):
    return x * lax.rsqrt(jnp.mean(x * x, axis=-1, keepdims=True) + EPS) * g


def _inproj_kernel(x_ref, g_ref, w_ref, cos_ref, sa_ref, sb_ref, lbp_ref,
                   u_ref, q_ref, k_ref, v_ref, qq_ref, kk_ref, lf_ref, vv_ref):
    hb = _rms(x_ref[...], g_ref[...]).astype(BF16)

    def proj(c):
        return _dot(hb, w_ref[:, c * W_BR:(c + 1) * W_BR])

    u_ref[...] = proj(0) * _sigmoid(proj(1))

    cos = cos_ref[...]
    sa = sa_ref[...]
    sb = sb_ref[...]

    def rope_to(dst_ref, y):
        for gi in range(W_BR // LANES):
            yg = y[:, gi * LANES:(gi + 1) * LANES]
            dst_ref[:, gi * LANES:(gi + 1) * LANES] = (
                yg * cos + pltpu.roll(yg, LANES - ROPE_DIM // 2, 1) * sa + pltpu.roll(yg, ROPE_DIM // 2, 1) * sb)

    rope_to(q_ref, proj(2))
    rope_to(k_ref, proj(3))
    v_ref[...] = proj(4)

    cq = proj(5)
    qq_ref[...] = cq * _sigmoid(cq) * (HG_DK ** -0.5)

    fp = proj(6)
    log_lb = lbp_ref[0:1, :]
    log_1m_lb = lbp_ref[1:2, :]
    one_m_lb = lbp_ref[2:3, :]
    log_sig = jnp.minimum(fp, 0.0) - jnp.log1p(jnp.exp(-jnp.abs(fp)))
    b = log_1m_lb + log_sig
    lf_ref[...] = jnp.maximum(log_lb, b) + jnp.log1p(jnp.exp(-jnp.abs(log_lb - b)))
    kk_ref[...] = one_m_lb * _sigmoid(-fp)
    vv_ref[...] = proj(7)


def _inproj(x2d, norm_g, w1, cos, sa, sb, lbp, tm):
    n = x2d.shape[0]
    tm = min(tm, n)
    nrope = cos.shape[0] // tm
    row = lambda i: (i, 0)
    fixed = lambda i: (0, 0)
    rope_spec = pl.BlockSpec((tm, LANES), lambda i: (i % nrope, 0))
    return pl.pallas_call(
        _inproj_kernel,
        grid=(n // tm,),
        in_specs=[pl.BlockSpec((tm, D_MODEL), row), pl.BlockSpec((1, D_MODEL), fixed),
                  pl.BlockSpec(w1.shape, fixed), rope_spec, rope_spec, rope_spec,
                  pl.BlockSpec(lbp.shape, fixed)],
        out_specs=[pl.BlockSpec((tm, W_BR), row)] * 8,
        out_shape=[jax.ShapeDtypeStruct((n, W_BR), F32)] * 8,
        compiler_params=_params(("parallel",)),
        name="inproj",
    )(x2d, norm_g, w1, cos, sa, sb, lbp)


CONV_HALO = 32


def _ln_swish(y, lg, lb):
    mu = jnp.mean(y, axis=-1, keepdims=True)
    d = y - mu
    var = jnp.mean(d * d, axis=-1, keepdims=True)
    yn = d * lax.rsqrt(var + EPS) * lg + lb
    return yn * _sigmoid(yn)


CONV_PITCH = 2
CONV_NORM_ROWS = 128


def _conv_prompt_kernel(cur_ref, halo_ref, w_ref, b_ref, lg_ref, lb_ref, o_ref, ext_ref, cv_ref, *, tt, rc):
    i = pl.program_id(1)
    ngrp = W_BR // LANES
    nblk = rc // SUBLANES
    halo = jnp.where(i > 0, halo_ref[0], 0.0)
    off = CONV_HALO - (CONV_W - 1)
    for g in range(ngrp):
        lanes = slice(g * LANES, (g + 1) * LANES)
        ext_ref[g, pl.ds(0, CONV_HALO, stride=CONV_PITCH), :] = halo[:, lanes]
        ext_ref[g, pl.ds(CONV_PITCH * CONV_HALO, tt, stride=CONV_PITCH), :] = cur_ref[0, :, lanes]
        taps = [jnp.broadcast_to(w_ref[j:j + 1, lanes], (SUBLANES, LANES)) for j in range(CONV_W)]

        def body(c, carry, g=g, lanes=lanes, taps=taps):
            r0 = pl.multiple_of(c * rc, rc)
            acc = [None] * nblk
            for s in range(off, off + rc - SUBLANES + CONV_W):
                win = ext_ref[g, pl.ds(CONV_PITCH * (r0 + s), SUBLANES, stride=CONV_PITCH), :]
                for b in range(nblk):
                    j = s - off - SUBLANES * b
                    if 0 <= j < CONV_W:
                        term = taps[j] * win
                        acc[b] = term if acc[b] is None else acc[b] + term
            cv_ref[pl.ds(r0, rc), lanes] = jnp.concatenate(acc, axis=0)
            return carry

        lax.fori_loop(0, tt // rc, body, 0)

    def norm(c, carry):
        r0 = pl.multiple_of(c * CONV_NORM_ROWS, CONV_NORM_ROWS)
        y = cv_ref[pl.ds(r0, CONV_NORM_ROWS), :] + b_ref[...]
        o_ref[0, pl.ds(r0, CONV_NORM_ROWS), :] = _ln_swish(y, lg_ref[...], lb_ref[...])
        return carry

    lax.fori_loop(0, tt // CONV_NORM_ROWS, norm, 0)


def _conv_prompt(u, conv_w, conv_b, ln_g, ln_b, tt=512, rc=64):
    b, t, _ = u.shape
    per_tile = tt // CONV_HALO
    fixed = lambda bi, i: (0, 0)
    return pl.pallas_call(
        functools.partial(_conv_prompt_kernel, tt=tt, rc=rc),
        grid=(b, t // tt),
        in_specs=[pl.BlockSpec((1, tt, W_BR), lambda bi, i: (bi, i, 0)),
                  pl.BlockSpec((1, CONV_HALO, W_BR), lambda bi, i: (bi, jnp.maximum(i * per_tile - 1, 0), 0)),
                  pl.BlockSpec((CONV_W, W_BR), fixed), pl.BlockSpec((1, W_BR), fixed),
                  pl.BlockSpec((1, W_BR), fixed), pl.BlockSpec((1, W_BR), fixed)],
        out_specs=pl.BlockSpec((1, tt, W_BR), lambda bi, i: (bi, i, 0)),
        out_shape=jax.ShapeDtypeStruct((b, t, W_BR), F32),
        scratch_shapes=[pltpu.VMEM((W_BR // LANES, CONV_PITCH * (CONV_HALO + tt), LANES), F32),
                        pltpu.VMEM((tt, W_BR), F32)],
        compiler_params=_params(("parallel", "parallel")),
        name="conv_prompt",
    )(u, u, conv_w, conv_b, ln_g, ln_b)


def _conv_decode_kernel(cache_ref, u_ref, w_ref, b_ref, lg_ref, lb_ref, y_ref, nc_ref, *, ts):
    hist = CONV_W - 1

    def slab(r):
        return cache_ref[0, r] if r < hist else u_ref[r - hist]

    for t in range(ts):
        acc = w_ref[0:1, :] * slab(t)
        for j in range(1, CONV_W):
            acc = acc + w_ref[j:j + 1, :] * slab(t + j)
        y_ref[t] = _ln_swish(acc + b_ref[...], lg_ref[...], lb_ref[...])
    for r in range(hist):
        nc_ref[r] = slab(r + ts)


def _conv_decode(cache_t, layer, u_t, conv_w, conv_b, ln_g, ln_b, bb=32):
    ts, bs, _ = u_t.shape
    bb = min(bb, bs)
    hist = CONV_W - 1
    fixed = lambda i: (0, 0)
    return pl.pallas_call(
        functools.partial(_conv_decode_kernel, ts=ts),
        grid=(bs // bb,),
        in_specs=[pl.BlockSpec((1, hist, bb, W_BR), lambda i: (layer, 0, i, 0)),
                  pl.BlockSpec((ts, bb, W_BR), lambda i: (0, i, 0)),
                  pl.BlockSpec((CONV_W, W_BR), fixed), pl.BlockSpec((1, W_BR), fixed),
                  pl.BlockSpec((1, W_BR), fixed), pl.BlockSpec((1, W_BR), fixed)],
        out_specs=[pl.BlockSpec((ts, bb, W_BR), lambda i: (0, i, 0)),
                   pl.BlockSpec((hist, bb, W_BR), lambda i: (0, i, 0))],
        out_shape=[jax.ShapeDtypeStruct((ts, bs, W_BR), F32),
                   jax.ShapeDtypeStruct((hist, bs, W_BR), F32)],
        compiler_params=_params(("parallel",)),
        name="conv_decode",
    )(cache_t, u_t, conv_w, conv_b, ln_g, ln_b)


ATT_BLK = 128
ATT_SKEW_SM = 3
ATT_SKEW_OUT = 6


def _run_skewed(units):
    n = len(units)
    for j in range(n + ATT_SKEW_OUT):
        if j < n:
            units[j][0]()
        if 0 <= j - ATT_SKEW_SM < n:
            units[j - ATT_SKEW_SM][1]()
        if 0 <= j - ATT_SKEW_OUT < n:
            units[j - ATT_SKEW_OUT][2]()


def _interleave(major, minor):
    if not minor:
        return list(major)
    every = -(-len(major) // len(minor))
    out, rest = [], list(minor)
    for i, unit in enumerate(major):
        out.append(unit)
        if (i + 1) % every == 0 and rest:
            out.append(rest.pop(0))
    return out + rest


def _attn_prompt_units(q_ref, k_ref, v_ref, os_ref, ls_ref, it, *, t, blocks):
    trips = t // ATT_BLK // blocks
    r = lax.broadcasted_iota(jnp.int32, (ATT_BLK, ATT_BLK), 0)
    c = lax.broadcasted_iota(jnp.int32, (ATT_BLK, ATT_BLK), 1)
    own_ok = c <= r
    prev_ok = c >= r
    lane = lax.broadcasted_iota(jnp.int32, (ATT_BLK, LANES), 1)
    first_head = lane < ATT_HD
    scale = ATT_HD ** -0.5
    units = []

    for p, (_, dil) in enumerate(DIL_PATTERNS):
        nb = t // dil // ATT_BLK
        span = dil * ATT_BLK

        def rows(start, dil=dil):
            if dil == 1:
                return pl.ds(pl.multiple_of(start, ATT_BLK), ATT_BLK)
            return pl.ds(start, ATT_BLK, stride=dil)

        unroll = min(blocks, nb)
        per_cls = nb // unroll
        ncls = max(1, blocks // nb)
        assert dil // ncls * per_cls == trips

        cls0 = it // per_cls
        first = it - cls0 * per_cls
        has_prev = first > 0
        starts, cache, st = {}, {}, {}
        for g in range(ncls):
            start0 = cls0 * ncls + g + span * unroll * first
            starts[g, 0] = jnp.where(has_prev, start0 - span, start0)
            for u in range(unroll):
                starts[g, u + 1] = start0 + span * u

        def blk(ref, g, u, cache=cache, starts=starts, rows=rows):
            if (id(ref), g, u) not in cache:
                x = ref[0, rows(starts[g, u]), :].astype(BF16)
                if ref is v_ref:
                    x = jnp.concatenate([x, jnp.ones((ATT_BLK, LANES), BF16)], axis=1)
                cache[id(ref), g, u] = x
            return cache[id(ref), g, u]

        for g in range(ncls):
            for u in range(unroll):
                for hh in range(2):
                    key = (g, u, hh)

                    def scores(g=g, u=u, hh=hh, key=key, st=st, starts=starts, rows=rows, blk=blk, has_prev=has_prev):
                        if hh == 0:
                            st["q", g, u] = q_ref[0, rows(starts[g, u + 1]), :] * scale
                        head = first_head if hh == 0 else jnp.logical_not(first_head)
                        qh = jnp.where(head, st["q", g, u], 0.0).astype(BF16)
                        pmask = jnp.logical_and(prev_ok, has_prev) if u == 0 else prev_ok
                        st["so", key] = jnp.where(own_ok, _dot_nt(qh, blk(k_ref, g, u + 1)), NEG)
                        st["sp", key] = jnp.where(pmask, _dot_nt(qh, blk(k_ref, g, u)), NEG)

                    def softmax(key=key, st=st):
                        so, sp = st.pop(("so", key)), st.pop(("sp", key))
                        m = jnp.max(jnp.maximum(so, sp), axis=-1, keepdims=True)
                        st["p", key] = (jnp.exp(so - m).astype(BF16), jnp.exp(sp - m).astype(BF16))
                        st["m", key] = m

                    def output(p=p, g=g, u=u, hh=hh, key=key, st=st, starts=starts, rows=rows, blk=blk):
                        po, pp = st.pop(("p", key))
                        acc = _dot(po, blk(v_ref, g, u + 1)) + _dot(pp, blk(v_ref, g, u))
                        l = acc[:, LANES:2 * LANES]
                        st["o", key] = acc[:, 0:LANES] * (1.0 / l)
                        st["lse", key] = st.pop(("m", key)) + jnp.log(l)
                        if hh == 1:
                            dst = rows(starts[g, u + 1])
                            other = (g, u, 0)
                            os_ref[p, dst, :] = jnp.where(first_head, st.pop(("o", other)), st.pop(("o", key)))
                            ls_ref[p, dst, :] = jnp.where(first_head, st.pop(("lse", other)), st.pop(("lse", key)))

                    units.append((scores, softmax, output))
    return units


def _attn_prompt_merge(o_ref, os_ref, ls_ref, *, t, rows_c):
    def combine(i, carry):
        r0 = pl.multiple_of(i * rows_c, rows_c)
        ls = [ls_ref[p, pl.ds(r0, rows_c), :] for p in range(3)]
        mx = jnp.maximum(jnp.maximum(ls[0], ls[1]), ls[2])
        ws = [jnp.exp(x - mx) for x in ls]
        num = ws[0] * os_ref[0, pl.ds(r0, rows_c), :]
        num = num + ws[1] * os_ref[1, pl.ds(r0, rows_c), :]
        num = num + ws[2] * os_ref[2, pl.ds(r0, rows_c), :]
        o_ref[0, pl.ds(r0, rows_c), :] = num / (ws[0] + ws[1] + ws[2])
        return carry

    lax.fori_loop(0, t // rows_c, combine, 0)


DEC_Q = 8
DEC_SROWS = 32
DEC_HB = 4


def _attn_decode_units(q_ref, kn_ref, vn_ref, kc_ref, vc_ref, bias_ref, o_ref, ko_ref, vo_ref, *, ts):
    nrow = kc_ref.shape[-1]
    ext = nrow + LANES
    npat = len(DIL_PATTERNS)
    zl = jnp.zeros((DEC_Q, LANES - ATT_HD), F32)
    zr = jnp.zeros((LANES - DEC_Q, LANES), F32)

    def new_t(x):
        x = jnp.concatenate([jnp.concatenate([x, zl], axis=1), zr], axis=0)
        return x.T[0:ATT_HD, :]

    units = []
    for hh in range(DEC_HB):
        st = {}

        def scores(hh=hh, st=st):
            kext = jnp.concatenate([kc_ref[0, 0, hh], new_t(kn_ref[0, hh])], axis=1)
            st["v"] = jnp.concatenate([vc_ref[0, 0, hh], new_t(vn_ref[0, hh])], axis=1).astype(BF16)
            ko_ref[0, 0, hh] = pltpu.roll(kext, ext - ts, 1)[:, 0:nrow]
            vext = jnp.concatenate([vc_ref[0, 0, hh], new_t(vn_ref[0, hh])], axis=1)
            vo_ref[0, 0, hh] = pltpu.roll(vext, ext - ts, 1)[:, 0:nrow]
            qh = (q_ref[0, hh] * (ATT_HD ** -0.5)).astype(BF16)
            st["s"] = _dot(qh, kext.astype(BF16)) + bias_ref[...]

        def softmax(st=st):
            s = st.pop("s")
            st["m"] = jnp.max(s, axis=-1, keepdims=True)
            pexp = jnp.exp(s - st["m"])
            st["l"] = jnp.sum(pexp, axis=-1, keepdims=True)
            st["p"] = pexp.astype(BF16)

        def output(hh=hh, st=st):
            acc = _dot_nt(st.pop("p"), st.pop("v"))
            m, l = st.pop("m"), st.pop("l")
            mx = jnp.maximum(jnp.maximum(m[0:DEC_Q], m[DEC_Q:2 * DEC_Q]), m[2 * DEC_Q:3 * DEC_Q])
            num = jnp.zeros((DEC_Q, ATT_HD), F32)
            den = jnp.zeros((DEC_Q, 1), F32)
            for p in range(npat):
                w = jnp.exp(m[p * DEC_Q:(p + 1) * DEC_Q] - mx)
                num = num + w * acc[p * DEC_Q:(p + 1) * DEC_Q]
                den = den + w * l[p * DEC_Q:(p + 1) * DEC_Q]
            o_ref[0, hh] = num / den

        units.append((scores, softmax, output))
    return units


def _attn_kernel(q3_ref, kn_ref, vn_ref, kc_ref, vc_ref, bias_ref, qp_ref, kp_ref, vp_ref, *rest,
                 ts, t, rows_c, sub_steps, aliased):
    if aliased:
        rest = rest[2:]
    od_ref, ko_ref, vo_ref, op_ref, os_ref, ls_ref = rest
    sub = pl.program_id(0) % sub_steps
    prompt = _attn_prompt_units(qp_ref, kp_ref, vp_ref, os_ref, ls_ref, sub, t=t, blocks=t // ATT_BLK // sub_steps)
    decode = _attn_decode_units(q3_ref, kn_ref, vn_ref, kc_ref, vc_ref, bias_ref, od_ref, ko_ref, vo_ref, ts=ts)
    _run_skewed(_interleave(prompt, decode))

    @pl.when(sub == sub_steps - 1)
    def _():
        _attn_prompt_merge(op_ref, os_ref, ls_ref, t=t, rows_c=rows_c)


def _attn(q3, kn, vn, cache_kt, cache_vt, bias, layer, prev_out, ts, qp, kp, vp, rows_c=256):
    bs = q3.shape[0]
    nrow = cache_kt.shape[-1]
    b, t, _ = qp.shape
    ngrp = ATT_HEADS // DEC_HB
    npair = W_BR // LANES
    nsteps = bs * ngrp
    sub_steps = nsteps // (b * npair)
    assert sub_steps * b * npair == nsteps and t // ATT_BLK % sub_steps == 0
    cache = pl.BlockSpec((1, 1, DEC_HB, ATT_HD, nrow), lambda i: (layer, i // ngrp, i % ngrp, 0, 0))
    new = pl.BlockSpec((1, DEC_HB, DEC_Q, ATT_HD), lambda i: (i // ngrp, i % ngrp, 0, 0))
    prompt = pl.BlockSpec((1, t, LANES), lambda i: (i // sub_steps // npair, 0, i // sub_steps % npair))
    in_specs = [pl.BlockSpec((1, DEC_HB, DEC_SROWS, ATT_HD), lambda i: (i // ngrp, i % ngrp, 0, 0)), new, new,
                cache, cache, pl.BlockSpec(bias.shape, lambda i: (0, 0)), prompt, prompt, prompt]
    args = [q3, kn, vn, cache_kt, cache_vt, bias, qp, kp, vp]
    aliases = {}
    if prev_out is not None:
        in_specs += [pl.BlockSpec(memory_space=pl.ANY)] * 2
        aliases = {len(args): 1, len(args) + 1: 2}
        args += list(prev_out)
    return pl.pallas_call(
        functools.partial(_attn_kernel, ts=ts, t=t, rows_c=rows_c, sub_steps=sub_steps,
                          aliased=prev_out is not None),
        grid=(nsteps,),
        in_specs=in_specs,
        out_specs=[new, cache, cache, prompt],
        out_shape=[jax.ShapeDtypeStruct((bs, ATT_HEADS, DEC_Q, ATT_HD), F32),
                   jax.ShapeDtypeStruct(cache_kt.shape, F32), jax.ShapeDtypeStruct(cache_vt.shape, F32),
                   jax.ShapeDtypeStruct((b, t, W_BR), F32)],
        scratch_shapes=[pltpu.VMEM((3, t, LANES), F32), pltpu.VMEM((3, t, LANES), F32)],
        input_output_aliases=aliases,
        compiler_params=_params(("arbitrary",)),
        name="attn",
    )(*args)


def _decode_bias(nrow, ts):
    ri = jnp.arange(DEC_SROWS)
    pat, qt = ri // DEC_Q, ri % DEC_Q
    wins = jnp.array([w for w, _ in DIL_PATTERNS] + [0], jnp.int32)[pat]
    dils = jnp.array([d for _, d in DIL_PATTERNS] + [1], jnp.int32)[pat]
    real_row = (pat < len(DIL_PATTERNS)) & (qt < ts)
    ci = jnp.arange(nrow + LANES)
    key_real = ci < nrow + ts
    delta = (nrow + qt)[:, None] - ci[None, :]
    ok = (delta >= 0) & (delta % dils[:, None] == 0) & (delta <= wins[:, None]) & key_real[None, :]
    ok = ok | ~real_row[:, None]
    return jnp.where(ok, 0.0, NEG).astype(F32)


HG_TILE = 128


def _hgrn_kernel(q_ref, k_ref, g_ref, v_ref, s0_ref, ng_ref, *rest, seg, aliased):
    o_ref, so_ref, st_ref = rest[1:] if aliased else rest
    c = HG_TILE
    nseg = c // seg
    t = pl.program_id(1)

    @pl.when(t == 0)
    def _():
        for j in range(nseg):
            for h in range(HG_HEADS):
                st_ref[j, h] = s0_ref[0, j, h].T

    q = q_ref[...]
    k = k_ref[...]
    g = g_ref[...]
    v = v_ref[...]
    row = lax.broadcasted_iota(jnp.int32, (c, c), 0)
    col = lax.broadcasted_iota(jnp.int32, (c, c), 1)
    rowi = lax.broadcasted_iota(jnp.int32, (c, W_BR), 0)
    seg_shift = seg.bit_length() - 1
    same_seg = (row >> seg_shift) == (col >> seg_shift)

    tri = jnp.where(jnp.logical_and(row >= col, same_seg), 1.0, 0.0).astype(BF16)
    g_hi = g.astype(BF16)
    r1 = g - g_hi.astype(F32)
    g_mid = r1.astype(BF16)
    g_lo = (r1 - g_mid.astype(F32)).astype(BF16)
    gc = (_dot(tri, g_hi) + _dot(tri, g_mid) + _dot(tri, g_lo)) * LOG2E

    def seg_last(x):
        x3 = x.reshape(nseg, seg, W_BR)
        return jnp.broadcast_to(x3[:, seg - 1:seg, :], x3.shape).reshape(c, W_BR)

    g_end = seg_last(gc)
    q_dec = (q * jnp.exp2(gc)).astype(BF16)
    k_dec = (k * jnp.exp2(g_end - gc)).astype(BF16)
    vb = v.astype(BF16)

    def head(x, h):
        return x[:, h * HG_DK:(h + 1) * HG_DK]

    qb = q.astype(BF16)
    kb = k.astype(BF16)
    a_acc = [jnp.where(row == col, _dot_nt(head(qb, h), head(kb, h)), 0.0) for h in range(HG_HEADS)]

    hlev = 1
    while hlev < seg:
        blk = 2 * hlev
        if hlev >= SUBLANES:
            nblk = c // blk
            g3 = gc.reshape(nblk, blk, W_BR)
            ref = g3[:, hlev - 1:hlev, :]
            k_lo = k.reshape(nblk, blk, W_BR)[:, 0:hlev] * jnp.exp2(ref - g3[:, 0:hlev])
            q_up = q.reshape(nblk, blk, W_BR)[:, hlev:blk] * jnp.exp2(g3[:, hlev:blk] - ref)
            zero = jnp.zeros((nblk, hlev, W_BR), F32)
            q_lv = jnp.concatenate([zero, q_up], axis=1).reshape(c, W_BR).astype(BF16)
            k_lv = jnp.concatenate([k_lo, zero], axis=1).reshape(c, W_BR).astype(BF16)
        else:
            if hlev == 4:
                g3 = gc.reshape(c // blk, blk, W_BR)
                ref = jnp.broadcast_to(g3[:, hlev - 1:hlev, :], g3.shape).reshape(c, W_BR)
            elif hlev == 1:
                ref = jnp.where((rowi & 1) == 1, pltpu.roll(gc, 1, 0), gc)
            else:
                pos = rowi & 3
                ref = jnp.where(pos == 0, pltpu.roll(gc, c - 1, 0),
                                jnp.where(pos == 1, gc,
                                          jnp.where(pos == 2, pltpu.roll(gc, 1, 0), pltpu.roll(gc, 2, 0))))
            x = jnp.exp2(-jnp.abs(gc - ref))
            upper = (rowi & (blk - 1)) >= hlev
            q_lv = jnp.where(upper, q * x, 0.0).astype(BF16)
            k_lv = jnp.where(upper, 0.0, k * x).astype(BF16)
        shift = blk.bit_length() - 1
        same_blk = (row >> shift) == (col >> shift)
        for h in range(HG_HEADS):
            a_acc[h] = a_acc[h] + jnp.where(same_blk, _dot_nt(head(q_lv, h), head(k_lv, h)), 0.0)
        hlev = blk

    d_end = jnp.exp2(g_end)
    for h in range(HG_HEADS):
        o_h = _dot(a_acc[h].astype(BF16), head(vb, h))
        vt = head(v, h).T.astype(BF16)
        kd = head(k_dec, h)
        inter = []
        for j in range(nseg):
            st = st_ref[j, h]
            inter.append(_dot_nt(head(q_dec, h)[j * seg:(j + 1) * seg, :], st.astype(BF16)))
            kd_j = kd if nseg == 1 else jnp.where((rowi[:, :HG_DK] >> seg_shift) == j, kd, jnp.zeros_like(kd))
            dj = head(d_end, h)[j * seg:j * seg + 1, :]
            st_ref[j, h] = st * dj + _dot(vt, kd_j)
        o_h = o_h + (inter[0] if nseg == 1 else jnp.concatenate(inter, axis=0))
        o_ref[:, h * HG_DV:(h + 1) * HG_DV] = _rms(o_h, head(ng_ref[...], h))

    @pl.when(t == pl.num_programs(1) - 1)
    def _():
        for j in range(nseg):
            for h in range(HG_HEADS):
                so_ref[0, j, h] = st_ref[j, h].T


def _hgrn(qq, kk, lf, vv, s0, layer_in, norm_g, groups, steps, seg, depth, layer_out, prev_states):
    n = qq.shape[0]
    nseg = HG_TILE // seg
    tile = pl.BlockSpec((HG_TILE, W_BR), lambda gi, ti: (gi * steps + ti, 0))
    in_specs = [tile, tile, tile, tile,
                pl.BlockSpec((1, nseg, HG_HEADS, HG_DK, HG_DV), lambda gi, ti: (layer_in, gi, 0, 0, 0)),
                pl.BlockSpec((1, W_BR), lambda gi, ti: (0, 0))]
    args = [qq, kk, lf, vv, s0, norm_g]
    aliases = {}
    if prev_states is not None:
        in_specs.append(pl.BlockSpec(memory_space=pl.ANY))
        aliases = {len(args): 1}
        args.append(prev_states)
    return pl.pallas_call(
        functools.partial(_hgrn_kernel, seg=seg, aliased=prev_states is not None),
        grid=(groups, steps),
        in_specs=in_specs,
        out_specs=[tile, pl.BlockSpec((1, nseg, HG_HEADS, HG_DK, HG_DV), lambda gi, ti: (layer_out, gi, 0, 0, 0))],
        out_shape=[jax.ShapeDtypeStruct((n, W_BR), F32),
                   jax.ShapeDtypeStruct((depth, groups * nseg, HG_HEADS, HG_DK, HG_DV), F32)],
        scratch_shapes=[pltpu.VMEM((nseg, HG_HEADS, HG_DV, HG_DK), F32)],
        input_output_aliases=aliases,
        compiler_params=_params(("parallel", "arbitrary")),
        name="hgrn",
    )(*args)


def _merge_kernel(x_ref, ya_ref, yb_ref, yc_ref, g_ref, w4_ref, wbr_ref, bg_ref, wo_ref, fg_ref, o_ref, *, final):
    x = x_ref[...]
    hb = _rms(x, g_ref[...]).astype(BF16)
    gate0 = 3 * W_BR
    merged = jnp.zeros(x.shape, F32)
    for p, y_ref in enumerate((ya_ref, yb_ref, yc_ref)):
        z = _dot(hb, w4_ref[:, p * W_BR:(p + 1) * W_BR])
        ys = (y_ref[...] * (z * _sigmoid(z))).astype(BF16)
        br = _dot(ys, wbr_ref[p])
        gl = _dot(hb, w4_ref[:, gate0 + p * D_MODEL:gate0 + (p + 1) * D_MODEL]) + bg_ref[p:p + 1, :]
        merged = merged + _sigmoid(gl) * br
    out = x + _dot(merged.astype(BF16), wo_ref[...])
    if final:
        out = _rms(out, fg_ref[...])
    o_ref[...] = out


def _merge(x2d, ya, yb, yc, norm_g, w4, wbr, b_gate, wo, final_g, final, tm):
    n = x2d.shape[0]
    tm = min(tm, n)
    row = lambda i: (i, 0)
    fixed = lambda i: (0, 0)
    ysp = pl.BlockSpec((tm, W_BR), row)
    return pl.pallas_call(
        functools.partial(_merge_kernel, final=final),
        grid=(n // tm,),
        in_specs=[pl.BlockSpec((tm, D_MODEL), row), ysp, ysp, ysp, pl.BlockSpec((1, D_MODEL), fixed),
                  pl.BlockSpec(w4.shape, fixed), pl.BlockSpec(wbr.shape, lambda i: (0, 0, 0)),
                  pl.BlockSpec(b_gate.shape, fixed), pl.BlockSpec(wo.shape, fixed),
                  pl.BlockSpec((1, D_MODEL), fixed)],
        out_specs=pl.BlockSpec((tm, D_MODEL), row),
        out_shape=jax.ShapeDtypeStruct((n, D_MODEL), F32),
        compiler_params=_params(("parallel",)),
        name="merge",
    )(x2d, ya, yb, yc, norm_g, w4, wbr, b_gate, wo, final_g)


def _rope_tables(pos):
    half = ROPE_DIM // 2
    freqs = jnp.power(jnp.float32(ROPE_THETA), -jnp.arange(half, dtype=F32) / half)
    ang = pos.astype(F32)[:, None] * freqs[None, :]
    cos, sin = jnp.cos(ang), jnp.sin(ang)
    rows = pos.shape[0]
    ones = jnp.ones((rows, ATT_HD - ROPE_DIM), F32)
    zeros = jnp.zeros((rows, ATT_HD - ROPE_DIM), F32)
    zh = jnp.zeros((rows, half), F32)
    c_head = jnp.concatenate([cos, cos, ones], axis=1)
    sa_head = jnp.concatenate([-sin, zh, zeros], axis=1)
    sb_head = jnp.concatenate([zh, sin, zeros], axis=1)
    rep = LANES // ATT_HD
    return tuple(jnp.tile(x, (1, rep)) for x in (c_head, sa_head, sb_head))


def kernel(x_prompt, x_sample, cache_conv, cache_win_k, cache_win_v, state_hgrn, norm_g, w_in, conv_w, conv_b, ln_a_g, ln_a_b, hg_lb_param, hg_norm_g, w_branch, b_gate, w_out, final_norm_g):
    b, t, d = x_prompt.shape
    bs, ts, _ = x_sample.shape
    depth = w_in.shape[0]
    nrow = cache_win_k.shape[2]
    tm = 512

    lb_all = jnp.cumsum(jax.nn.softmax(hg_lb_param.astype(F32), axis=0), axis=0)
    lb_all = lb_all - lb_all[:1]

    rope_p = _rope_tables(jnp.arange(t, dtype=jnp.int32))
    rope_s = _rope_tables(PAST_LEN + jnp.arange(bs * ts, dtype=jnp.int32) % ts)
    bias = _decode_bias(nrow, ts)
    zero_state = jnp.zeros((1, b, HG_HEADS, HG_DK, HG_DV), F32)
    cache_kt = jnp.transpose(cache_win_k, (0, 1, 3, 4, 2))
    cache_vt = jnp.transpose(cache_win_v, (0, 1, 3, 4, 2))
    cache_conv_t = jnp.transpose(cache_conv, (0, 2, 1, 3))

    def head_major(a):
        a = jnp.transpose(a.reshape(bs, ts, ATT_HEADS, ATT_HD), (0, 2, 1, 3))
        return jnp.pad(a, ((0, 0), (0, 0), (0, DEC_Q - ts), (0, 0)))

    xp = x_prompt.reshape(b * t, d)
    xs = x_sample.reshape(bs * ts, d)
    conv_p, k_p, v_p, conv_s = [], [], [], []
    s_p = s_s = None
    kv_out = None
    seg_s = 16
    for l in range(depth):
        wl = w_in[l]
        w1 = jnp.concatenate([wl[:, 0:2 * W_BR], wl[:, 3 * W_BR:6 * W_BR], wl[:, 7 * W_BR:10 * W_BR]], axis=1).astype(BF16)
        w4 = jnp.concatenate([wl[:, 2 * W_BR:3 * W_BR], wl[:, 6 * W_BR:7 * W_BR], wl[:, 10 * W_BR:11 * W_BR],
                              wl[:, 11 * W_BR:]], axis=1).astype(BF16)
        wbr = w_branch[l].astype(BF16)
        wo = w_out[l].astype(BF16)
        lb = lb_all[l]
        lbp = jnp.stack([jnp.log(lb), jnp.log1p(-lb), 1.0 - lb])
        ng = norm_g[l][None]
        cw, cb, lg, lbias = conv_w[l], conv_b[l][None], ln_a_g[l][None], ln_a_b[l][None]
        hng = hg_norm_g[l][None]

        up, qp, kp, vp, qqp, kkp, lfp, vvp = _inproj(xp, ng, w1, *rope_p, lbp, tm)
        us, qs, ks, vs, qqs, kks, lfs, vvs = _inproj(xs, ng, w1, *rope_s, lbp, tm)

        up3 = up.reshape(b, t, W_BR)
        ya_p = _conv_prompt(up3, cw, cb, lg, lbias).reshape(b * t, W_BR)
        conv_p.append(up3[:, t - (CONV_W - 1):])
        us_t = jnp.transpose(us.reshape(bs, ts, W_BR), (1, 0, 2))
        ya_st, nc = _conv_decode(cache_conv_t, l, us_t, cw, cb, lg, lbias)
        ya_s = jnp.transpose(ya_st, (1, 0, 2)).reshape(bs * ts, W_BR)
        conv_s.append(nc)

        qp3, kp3, vp3 = (a.reshape(b, t, W_BR) for a in (qp, kp, vp))
        keep = min(WIN_MAX, t)
        k_p.append(kp3[:, t - keep:].reshape(b, keep, ATT_HEADS, ATT_HD))
        v_p.append(vp3[:, t - keep:].reshape(b, keep, ATT_HEADS, ATT_HD))
        q8, kn8, vn8 = (head_major(a) for a in (qs, ks, vs))
        q3 = jnp.pad(jnp.tile(q8, (1, 1, len(DIL_PATTERNS), 1)), ((0, 0), (0, 0), (0, DEC_SROWS - 3 * DEC_Q), (0, 0)))
        of, k_out, v_out, yb_p3 = _attn(q3, kn8, vn8, cache_kt, cache_vt, bias, l, kv_out, ts, qp3, kp3, vp3)
        kv_out = (k_out, v_out)
        yb_p = yb_p3.reshape(b * t, W_BR)
        yb_s = jnp.transpose(of[:, :, :ts], (0, 2, 1, 3)).reshape(bs * ts, W_BR)

        yc_p, s_p = _hgrn(qqp, kkp, lfp, vvp, zero_state, 0, hng, b, t // HG_TILE, HG_TILE, depth, l, s_p)

        def pad_seg(a):
            return jnp.pad(a.reshape(bs, ts, W_BR), ((0, 0), (0, seg_s - ts), (0, 0))).reshape(bs * seg_s, W_BR)

        yc_s16, s_s = _hgrn(pad_seg(qqs), pad_seg(kks), pad_seg(lfs), pad_seg(vvs), state_hgrn, l, hng,
                            bs * seg_s // HG_TILE, 1, seg_s, depth, l, s_s)
        yc_s = yc_s16.reshape(bs, seg_s, W_BR)[:, :ts].reshape(bs * ts, W_BR)

        final = l == depth - 1
        fg = final_norm_g[None]
        xp = _merge(xp, ya_p, yb_p, yc_p, ng, w4, wbr, b_gate[l], wo, fg, final, tm)
        xs = _merge(xs, ya_s, yb_s, yc_s, ng, w4, wbr, b_gate[l], wo, fg, final, tm)

    return (xp.reshape(b, t, d), xs.reshape(bs, ts, d),
            jnp.stack(conv_p), jnp.stack(k_p), jnp.stack(v_p), s_p,
            jnp.transpose(jnp.stack(conv_s), (0, 2, 1, 3)),
            jnp.transpose(kv_out[0], (0, 1, 4, 2, 3)), jnp.transpose(kv_out[1], (0, 1, 4, 2, 3)), s_s)
```

```python
import functools

import jax
import jax.numpy as jnp
from jax import lax
from jax.experimental import pallas as pl
from jax.experimental.pallas import tpu as pltpu

F32 = jnp.float32
BF16 = jnp.bfloat16

D_MODEL = 1024
W_BR = 512
CONV_W = 31
ATT_HD = 64
ATT_HEADS = 8
ROPE_DIM = 16
ROPE_THETA = 500000.0
DIL_PATTERNS = ((128, 1), (512, 4), (2048, 16))
N_BACK = 128
WIN_MAX = 2048
PAST_LEN = 2048
HG_DK = 128
HG_DV = 128
HG_HEADS = 4
EPS = 1e-6
NEG = -1e30
LOG2E = 1.4426950408889634

LANES = 128
SUBLANES = 8
VMEM_LIMIT = 56 * 1024 * 1024
ATT_VMEM_LIMIT = 58 * 1024 * 1024


def _params(sem, vmem_limit=VMEM_LIMIT):
    return pltpu.CompilerParams(dimension_semantics=sem, vmem_limit_bytes=vmem_limit)


def _sigmoid(x):
    return 1.0 / (1.0 + jnp.exp(-x))


def _dot(a, b):
    return jnp.dot(a, b, preferred_element_type=F32)


def _dot_nt(a, b):
    return lax.dot_general(a, b, (((1,), (1,)), ((), ())), preferred_element_type=F32)


def _rms(x, g):
    return x * lax.rsqrt(jnp.mean(x * x, axis=-1, keepdims=True) + EPS) * g


def _inproj_kernel(x_ref, g_ref, w_ref, cos_ref, sa_ref, sb_ref, lbp_ref, *rest, window_tiles, seq_tiles, aliased):
    if aliased:
        rest = rest[2:]
    u_ref, q_ref, k_ref, v_ref, qq_ref, kk_ref, lf_ref, vv_ref = rest[:8]
    hb = _rms(x_ref[...], g_ref[...]).astype(BF16)
    tm = x_ref.shape[0]

    def proj(c):
        return _dot(hb, w_ref[:, c * W_BR:(c + 1) * W_BR])

    cos = cos_ref[...]
    sa = sa_ref[...]
    sb = sb_ref[...]

    def rope(y):
        out = []
        for gi in range(W_BR // LANES):
            yg = y[:, gi * LANES:(gi + 1) * LANES]
            out.append(yg * cos + pltpu.roll(yg, LANES - ROPE_DIM // 2, 1) * sa + pltpu.roll(yg, ROPE_DIM // 2, 1) * sb)
        return out

    def store(dst_ref, groups):
        for gi, yg in enumerate(groups):
            dst_ref[:, gi * LANES:(gi + 1) * LANES] = yg

    def store_time_minor(dst_ref, groups):
        pair = LANES // ATT_HD
        for gi, yg in enumerate(groups):
            for rb in range(tm // LANES):
                blk = yg[rb * LANES:(rb + 1) * LANES, :].T
                dst_ref[0, 0, gi * pair:(gi + 1) * pair, :, rb * LANES:(rb + 1) * LANES] = blk.reshape(pair, ATT_HD, LANES)

    u_ref[...] = proj(0) * _sigmoid(proj(1))
    store(q_ref, rope(proj(2)))
    kg = rope(proj(3))
    store(k_ref, kg)
    y_v = proj(4)
    v_ref[...] = y_v
    if window_tiles:
        kt_ref, vt_ref = rest[8:10]

        @pl.when(pl.program_id(0) % seq_tiles >= seq_tiles - window_tiles)
        def _():
            store_time_minor(kt_ref, kg)
            store_time_minor(vt_ref, [y_v[:, gi * LANES:(gi + 1) * LANES] for gi in range(W_BR // LANES)])

    cq = proj(5)
    qq_ref[...] = cq * _sigmoid(cq) * (HG_DK ** -0.5)
    fp = proj(6)
    vv_ref[...] = proj(7)
    log_lb = lbp_ref[0:1, :]
    log_1m_lb = lbp_ref[1:2, :]
    one_m_lb = lbp_ref[2:3, :]
    log_sig = jnp.minimum(fp, 0.0) - jnp.log1p(jnp.exp(-jnp.abs(fp)))
    b = log_1m_lb + log_sig
    lf_ref[...] = jnp.maximum(log_lb, b) + jnp.log1p(jnp.exp(-jnp.abs(log_lb - b)))
    kk_ref[...] = one_m_lb * _sigmoid(-fp)


def _inproj(x2d, norm_g, w1, cos, sa, sb, lbp, tm, window=None):
    n = x2d.shape[0]
    tm = min(tm, n)
    nrope = cos.shape[0] // tm
    row = lambda i: (i, 0)
    fixed = lambda i: (0, 0)
    rope_spec = pl.BlockSpec((tm, LANES), lambda i: (i % nrope, 0))
    in_specs = [pl.BlockSpec((tm, D_MODEL), row), pl.BlockSpec((1, D_MODEL), fixed),
                pl.BlockSpec(w1.shape, fixed), rope_spec, rope_spec, rope_spec, pl.BlockSpec(lbp.shape, fixed)]
    args = [x2d, norm_g, w1, cos, sa, sb, lbp]
    out_specs = [pl.BlockSpec((tm, W_BR), row)] * 8
    out_shape = [jax.ShapeDtypeStruct((n, W_BR), F32)] * 8
    window_tiles = seq_tiles = 0
    aliases = {}
    if window is not None:
        seq_len, keep, depth, layer, prev = window
        seq_tiles, window_tiles = seq_len // tm, keep // tm
        first = seq_tiles - window_tiles
        wspec = pl.BlockSpec((1, 1, ATT_HEADS, ATT_HD, tm),
                             lambda i: (layer, i // seq_tiles, 0, 0, jnp.maximum(i % seq_tiles - first, 0)))
        out_specs = out_specs + [wspec, wspec]
        out_shape = out_shape + [jax.ShapeDtypeStruct((depth, n // seq_len, ATT_HEADS, ATT_HD, keep), F32)] * 2
        if prev is not None:
            in_specs += [pl.BlockSpec(memory_space=pl.ANY)] * 2
            aliases = {len(args): 8, len(args) + 1: 9}
            args += list(prev)
    return pl.pallas_call(
        functools.partial(_inproj_kernel, window_tiles=window_tiles, seq_tiles=seq_tiles, aliased=bool(aliases)),
        grid=(n // tm,),
        in_specs=in_specs,
        out_specs=out_specs,
        out_shape=out_shape,
        input_output_aliases=aliases,
        compiler_params=_params(("arbitrary",)),
        name="inproj",
    )(*args)


CONV_HALO = 32


def _ln_swish(y, lg, lb):
    mu = jnp.mean(y, axis=-1, keepdims=True)
    d = y - mu
    var = jnp.mean(d * d, axis=-1, keepdims=True)
    yn = d * lax.rsqrt(var + EPS) * lg + lb
    return yn * _sigmoid(yn)


CONV_PITCH = 2
CONV_NORM_ROWS = 128


def _conv_prompt_kernel(cur_ref, halo_ref, w_ref, b_ref, lg_ref, lb_ref, o_ref, ext_ref, cv_ref, *, tt, rc):
    i = pl.program_id(1)
    ngrp = W_BR // LANES
    nblk = rc // SUBLANES
    halo = jnp.where(i > 0, halo_ref[0], 0.0)
    off = CONV_HALO - (CONV_W - 1)
    for g in range(ngrp):
        lanes = slice(g * LANES, (g + 1) * LANES)
        ext_ref[g, pl.ds(0, CONV_HALO, stride=CONV_PITCH), :] = halo[:, lanes]
        ext_ref[g, pl.ds(CONV_PITCH * CONV_HALO, tt, stride=CONV_PITCH), :] = cur_ref[0, :, lanes]
        taps = [jnp.broadcast_to(w_ref[j:j + 1, lanes], (SUBLANES, LANES)) for j in range(CONV_W)]

        def body(c, carry, g=g, lanes=lanes, taps=taps):
            r0 = pl.multiple_of(c * rc, rc)
            acc = [None] * nblk
            for s in range(off, off + rc - SUBLANES + CONV_W):
                win = ext_ref[g, pl.ds(CONV_PITCH * (r0 + s), SUBLANES, stride=CONV_PITCH), :]
                for b in range(nblk):
                    j = s - off - SUBLANES * b
                    if 0 <= j < CONV_W:
                        term = taps[j] * win
                        acc[b] = term if acc[b] is None else acc[b] + term
            cv_ref[pl.ds(r0, rc), lanes] = jnp.concatenate(acc, axis=0)
            return carry

        lax.fori_loop(0, tt // rc, body, 0)

    def norm(c, carry):
        r0 = pl.multiple_of(c * CONV_NORM_ROWS, CONV_NORM_ROWS)
        y = cv_ref[pl.ds(r0, CONV_NORM_ROWS), :] + b_ref[...]
        o_ref[0, pl.ds(r0, CONV_NORM_ROWS), :] = _ln_swish(y, lg_ref[...], lb_ref[...])
        return carry

    lax.fori_loop(0, tt // CONV_NORM_ROWS, norm, 0)


def _conv_prompt(u, conv_w, conv_b, ln_g, ln_b, tt=512, rc=64):
    b, t, _ = u.shape
    per_tile = tt // CONV_HALO
    fixed = lambda bi, i: (0, 0)
    return pl.pallas_call(
        functools.partial(_conv_prompt_kernel, tt=tt, rc=rc),
        grid=(b, t // tt),
        in_specs=[pl.BlockSpec((1, tt, W_BR), lambda bi, i: (bi, i, 0)),
                  pl.BlockSpec((1, CONV_HALO, W_BR), lambda bi, i: (bi, jnp.maximum(i * per_tile - 1, 0), 0)),
                  pl.BlockSpec((CONV_W, W_BR), fixed), pl.BlockSpec((1, W_BR), fixed),
                  pl.BlockSpec((1, W_BR), fixed), pl.BlockSpec((1, W_BR), fixed)],
        out_specs=pl.BlockSpec((1, tt, W_BR), lambda bi, i: (bi, i, 0)),
        out_shape=jax.ShapeDtypeStruct((b, t, W_BR), F32),
        scratch_shapes=[pltpu.VMEM((W_BR // LANES, CONV_PITCH * (CONV_HALO + tt), LANES), F32),
                        pltpu.VMEM((tt, W_BR), F32)],
        compiler_params=_params(("parallel", "parallel")),
        name="conv_prompt",
    )(u, u, conv_w, conv_b, ln_g, ln_b)


def _conv_decode_kernel(cache_ref, u_ref, w_ref, b_ref, lg_ref, lb_ref, y_ref, nc_ref, *, ts):
    hist = CONV_W - 1

    def slab(r):
        return cache_ref[0, r] if r < hist else u_ref[r - hist]

    for t in range(ts):
        acc = w_ref[0:1, :] * slab(t)
        for j in range(1, CONV_W):
            acc = acc + w_ref[j:j + 1, :] * slab(t + j)
        y_ref[t] = _ln_swish(acc + b_ref[...], lg_ref[...], lb_ref[...])
    for r in range(hist):
        nc_ref[r] = slab(r + ts)


def _conv_decode(cache_t, layer, u_t, conv_w, conv_b, ln_g, ln_b, bb=32):
    ts, bs, _ = u_t.shape
    bb = min(bb, bs)
    hist = CONV_W - 1
    fixed = lambda i: (0, 0)
    return pl.pallas_call(
        functools.partial(_conv_decode_kernel, ts=ts),
        grid=(bs // bb,),
        in_specs=[pl.BlockSpec((1, hist, bb, W_BR), lambda i: (layer, 0, i, 0)),
                  pl.BlockSpec((ts, bb, W_BR), lambda i: (0, i, 0)),
                  pl.BlockSpec((CONV_W, W_BR), fixed), pl.BlockSpec((1, W_BR), fixed),
                  pl.BlockSpec((1, W_BR), fixed), pl.BlockSpec((1, W_BR), fixed)],
        out_specs=[pl.BlockSpec((ts, bb, W_BR), lambda i: (0, i, 0)),
                   pl.BlockSpec((hist, bb, W_BR), lambda i: (0, i, 0))],
        out_shape=[jax.ShapeDtypeStruct((ts, bs, W_BR), F32),
                   jax.ShapeDtypeStruct((hist, bs, W_BR), F32)],
        compiler_params=_params(("parallel",)),
        name="conv_decode",
    )(cache_t, u_t, conv_w, conv_b, ln_g, ln_b)


ATT_BLK = 128
ATT_SKEW_SM = 3
ATT_SKEW_OUT = 6


def _run_skewed(units):
    n = len(units)
    for j in range(n + ATT_SKEW_OUT):
        if j < n:
            units[j][0]()
        if 0 <= j - ATT_SKEW_SM < n:
            units[j - ATT_SKEW_SM][1]()
        if 0 <= j - ATT_SKEW_OUT < n:
            units[j - ATT_SKEW_OUT][2]()


def _interleave(major, minor):
    if not minor:
        return list(major)
    every = -(-len(major) // len(minor))
    out, rest = [], list(minor)
    for i, unit in enumerate(major):
        out.append(unit)
        if (i + 1) % every == 0 and rest:
            out.append(rest.pop(0))
    return out + rest


def _attn_prompt_units(q_ref, k_ref, v_ref, os_ref, ls_ref, it, *, t, blocks):
    trips = t // ATT_BLK // blocks
    r = lax.broadcasted_iota(jnp.int32, (ATT_BLK, ATT_BLK), 0)
    c = lax.broadcasted_iota(jnp.int32, (ATT_BLK, ATT_BLK), 1)
    own_ok = c <= r
    prev_ok = c >= r
    lane = lax.broadcasted_iota(jnp.int32, (ATT_BLK, LANES), 1)
    first_head = lane < ATT_HD
    scale = ATT_HD ** -0.5
    units = []

    for p, (_, dil) in enumerate(DIL_PATTERNS):
        nb = t // dil // ATT_BLK
        span = dil * ATT_BLK

        def rows(start, dil=dil):
            if dil == 1:
                return pl.ds(pl.multiple_of(start, ATT_BLK), ATT_BLK)
            return pl.ds(start, ATT_BLK, stride=dil)

        unroll = min(blocks, nb)
        per_cls = nb // unroll
        ncls = max(1, blocks // nb)
        assert dil // ncls * per_cls == trips

        cls0 = it // per_cls
        first = it - cls0 * per_cls
        has_prev = first > 0
        starts, cache, st = {}, {}, {}
        for g in range(ncls):
            start0 = cls0 * ncls + g + span * unroll * first
            starts[g, 0] = jnp.where(has_prev, start0 - span, start0)
            for u in range(unroll):
                starts[g, u + 1] = start0 + span * u

        def blk(ref, g, u, cache=cache, starts=starts, rows=rows):
            if (id(ref), g, u) not in cache:
                x = ref[0, rows(starts[g, u]), :].astype(BF16)
                if ref is v_ref:
                    x = jnp.concatenate([x, jnp.ones((ATT_BLK, LANES), BF16)], axis=1)
                cache[id(ref), g, u] = x
            return cache[id(ref), g, u]

        for g in range(ncls):
            for u in range(unroll):
                for hh in range(2):
                    key = (g, u, hh)

                    def scores(g=g, u=u, hh=hh, key=key, st=st, starts=starts, rows=rows, blk=blk, has_prev=has_prev):
                        if hh == 0:
                            st["q", g, u] = q_ref[0, rows(starts[g, u + 1]), :] * scale
                        head = first_head if hh == 0 else jnp.logical_not(first_head)
                        qh = jnp.where(head, st["q", g, u], 0.0).astype(BF16)
                        pmask = jnp.logical_and(prev_ok, has_prev) if u == 0 else prev_ok
                        st["so", key] = jnp.where(own_ok, _dot_nt(qh, blk(k_ref, g, u + 1)), NEG)
                        st["sp", key] = jnp.where(pmask, _dot_nt(qh, blk(k_ref, g, u)), NEG)

                    def softmax(key=key, st=st):
                        so, sp = st.pop(("so", key)), st.pop(("sp", key))
                        m = jnp.max(jnp.maximum(so, sp), axis=-1, keepdims=True)
                        st["p", key] = (jnp.exp(so - m).astype(BF16), jnp.exp(sp - m).astype(BF16))
                        st["m", key] = m

                    def output(p=p, g=g, u=u, hh=hh, key=key, st=st, starts=starts, rows=rows, blk=blk):
                        po, pp = st.pop(("p", key))
                        acc = _dot(po, blk(v_ref, g, u + 1)) + _dot(pp, blk(v_ref, g, u))
                        l = acc[:, LANES:2 * LANES]
                        st["o", key] = acc[:, 0:LANES] * (1.0 / l)
                        st["lse", key] = st.pop(("m", key)) + jnp.log(l)
                        if hh == 1:
                            dst = rows(starts[g, u + 1])
                            other = (g, u, 0)
                            os_ref[p, dst, :] = jnp.where(first_head, st.pop(("o", other)), st.pop(("o", key)))
                            ls_ref[p, dst, :] = jnp.where(first_head, st.pop(("lse", other)), st.pop(("lse", key)))

                    units.append((scores, softmax, output))
    return units


def _attn_prompt_merge(o_ref, os_ref, ls_ref, *, t, rows_c):
    def combine(i, carry):
        r0 = pl.multiple_of(i * rows_c, rows_c)
        ls = [ls_ref[p, pl.ds(r0, rows_c), :] for p in range(3)]
        mx = jnp.maximum(jnp.maximum(ls[0], ls[1]), ls[2])
        ws = [jnp.exp(x - mx) for x in ls]
        num = ws[0] * os_ref[0, pl.ds(r0, rows_c), :]
        num = num + ws[1] * os_ref[1, pl.ds(r0, rows_c), :]
        num = num + ws[2] * os_ref[2, pl.ds(r0, rows_c), :]
        o_ref[0, pl.ds(r0, rows_c), :] = num / (ws[0] + ws[1] + ws[2])
        return carry

    lax.fori_loop(0, t // rows_c, combine, 0)


DEC_Q = 8
DEC_SROWS = 32
DEC_HB = 8


def _attn_decode_units(q_ref, kn_ref, vn_ref, kc_ref, vc_ref, bias_ref, o_ref, ko_ref, vo_ref, *, ts):
    nrow = kc_ref.shape[-1]
    ext = nrow + LANES
    npat = len(DIL_PATTERNS)
    zl = jnp.zeros((DEC_Q, LANES - ATT_HD), F32)
    zr = jnp.zeros((LANES - DEC_Q, LANES), F32)

    def new_t(x):
        x = jnp.concatenate([jnp.concatenate([x, zl], axis=1), zr], axis=0)
        return x.T[0:ATT_HD, :]

    units = []
    for hh in range(DEC_HB):
        st = {}

        def scores(hh=hh, st=st):
            kext = jnp.concatenate([kc_ref[0, 0, hh], new_t(kn_ref[0, hh])], axis=1)
            st["v"] = jnp.concatenate([vc_ref[0, 0, hh], new_t(vn_ref[0, hh])], axis=1).astype(BF16)
            ko_ref[0, 0, hh] = pltpu.roll(kext, ext - ts, 1)[:, 0:nrow]
            vext = jnp.concatenate([vc_ref[0, 0, hh], new_t(vn_ref[0, hh])], axis=1)
            vo_ref[0, 0, hh] = pltpu.roll(vext, ext - ts, 1)[:, 0:nrow]
            qh = (q_ref[0, hh] * (ATT_HD ** -0.5)).astype(BF16)
            st["s"] = _dot(qh, kext.astype(BF16)) + bias_ref[...]

        def softmax(st=st):
            s = st.pop("s")
            st["m"] = jnp.max(s, axis=-1, keepdims=True)
            pexp = jnp.exp(s - st["m"])
            st["l"] = jnp.sum(pexp, axis=-1, keepdims=True)
            st["p"] = pexp.astype(BF16)

        def output(hh=hh, st=st):
            acc = _dot_nt(st.pop("p"), st.pop("v"))
            m, l = st.pop("m"), st.pop("l")
            mx = jnp.maximum(jnp.maximum(m[0:DEC_Q], m[DEC_Q:2 * DEC_Q]), m[2 * DEC_Q:3 * DEC_Q])
            num = jnp.zeros((DEC_Q, ATT_HD), F32)
            den = jnp.zeros((DEC_Q, 1), F32)
            for p in range(npat):
                w = jnp.exp(m[p * DEC_Q:(p + 1) * DEC_Q] - mx)
                num = num + w * acc[p * DEC_Q:(p + 1) * DEC_Q]
                den = den + w * l[p * DEC_Q:(p + 1) * DEC_Q]
            o_ref[0, hh] = num / den

        units.append((scores, softmax, output))
    return units


def _attn_kernel(q3_ref, kn_ref, vn_ref, kc_ref, vc_ref, bias_ref, qp_ref, kp_ref, vp_ref, *rest,
                 ts, t, rows_c, sub_steps, aliased):
    if aliased:
        rest = rest[2:]
    od_ref, ko_ref, vo_ref, op_ref, os_ref, ls_ref = rest
    sub = pl.program_id(0) % sub_steps
    prompt = _attn_prompt_units(qp_ref, kp_ref, vp_ref, os_ref, ls_ref, sub, t=t, blocks=t // ATT_BLK // sub_steps)
    decode = _attn_decode_units(q3_ref, kn_ref, vn_ref, kc_ref, vc_ref, bias_ref, od_ref, ko_ref, vo_ref, ts=ts)
    _run_skewed(_interleave(prompt, decode))

    @pl.when(sub == sub_steps - 1)
    def _():
        _attn_prompt_merge(op_ref, os_ref, ls_ref, t=t, rows_c=rows_c)


def _attn(q3, kn, vn, cache_kt, cache_vt, bias, layer, prev_out, ts, qp, kp, vp, rows_c=256):
    bs = q3.shape[0]
    nrow = cache_kt.shape[-1]
    b, t, _ = qp.shape
    ngrp = ATT_HEADS // DEC_HB
    npair = W_BR // LANES
    nsteps = bs * ngrp
    sub_steps = nsteps // (b * npair)
    assert sub_steps * b * npair == nsteps and t // ATT_BLK % sub_steps == 0
    cache = pl.BlockSpec((1, 1, DEC_HB, ATT_HD, nrow), lambda i: (layer, i // ngrp, i % ngrp, 0, 0))
    new = pl.BlockSpec((1, DEC_HB, DEC_Q, ATT_HD), lambda i: (i // ngrp, i % ngrp, 0, 0))
    prompt_map = lambda i: (i // sub_steps // npair, 0, i // sub_steps % npair)
    prompt = pl.BlockSpec((1, t, LANES), prompt_map)
    prompt_in = pl.BlockSpec((1, t, LANES), prompt_map, pipeline_mode=pl.Buffered(1))
    in_specs = [pl.BlockSpec((1, DEC_HB, DEC_SROWS, ATT_HD), lambda i: (i // ngrp, i % ngrp, 0, 0)), new, new,
                cache, cache, pl.BlockSpec(bias.shape, lambda i: (0, 0)), prompt_in, prompt_in, prompt_in]
    args = [q3, kn, vn, cache_kt, cache_vt, bias, qp, kp, vp]
    aliases = {}
    if prev_out is not None:
        in_specs += [pl.BlockSpec(memory_space=pl.ANY)] * 2
        aliases = {len(args): 1, len(args) + 1: 2}
        args += list(prev_out)
    return pl.pallas_call(
        functools.partial(_attn_kernel, ts=ts, t=t, rows_c=rows_c, sub_steps=sub_steps,
                          aliased=prev_out is not None),
        grid=(nsteps,),
        in_specs=in_specs,
        out_specs=[new, cache, cache, prompt],
        out_shape=[jax.ShapeDtypeStruct((bs, ATT_HEADS, DEC_Q, ATT_HD), F32),
                   jax.ShapeDtypeStruct(cache_kt.shape, F32), jax.ShapeDtypeStruct(cache_vt.shape, F32),
                   jax.ShapeDtypeStruct((b, t, W_BR), F32)],
        scratch_shapes=[pltpu.VMEM((3, t, LANES), F32), pltpu.VMEM((3, t, LANES), F32)],
        input_output_aliases=aliases,
        compiler_params=_params(("arbitrary",), ATT_VMEM_LIMIT),
        name="attn",
    )(*args)


def _decode_bias(nrow, ts):
    ri = jnp.arange(DEC_SROWS)
    pat, qt = ri // DEC_Q, ri % DEC_Q
    wins = jnp.array([w for w, _ in DIL_PATTERNS] + [0], jnp.int32)[pat]
    dils = jnp.array([d for _, d in DIL_PATTERNS] + [1], jnp.int32)[pat]
    real_row = (pat < len(DIL_PATTERNS)) & (qt < ts)
    ci = jnp.arange(nrow + LANES)
    key_real = ci < nrow + ts
    delta = (nrow + qt)[:, None] - ci[None, :]
    ok = (delta >= 0) & (delta % dils[:, None] == 0) & (delta <= wins[:, None]) & key_real[None, :]
    ok = ok | ~real_row[:, None]
    return jnp.where(ok, 0.0, NEG).astype(F32)


HG_TILE = 128


def _hgrn_kernel(q_ref, k_ref, g_ref, v_ref, s0_ref, ng_ref, *rest, seg, aliased):
    o_ref, so_ref, st_ref = rest[1:] if aliased else rest
    c = HG_TILE
    nseg = c // seg
    t = pl.program_id(1)

    @pl.when(t == 0)
    def _():
        for j in range(nseg):
            for h in range(HG_HEADS):
                st_ref[j, h] = s0_ref[0, j, h].T

    q = q_ref[...]
    k = k_ref[...]
    g = g_ref[...]
    v = v_ref[...]
    row = lax.broadcasted_iota(jnp.int32, (c, c), 0)
    col = lax.broadcasted_iota(jnp.int32, (c, c), 1)
    rowi = lax.broadcasted_iota(jnp.int32, (c, W_BR), 0)
    seg_shift = seg.bit_length() - 1
    same_seg = (row >> seg_shift) == (col >> seg_shift)

    tri = jnp.where(jnp.logical_and(row >= col, same_seg), 1.0, 0.0).astype(BF16)
    g_hi = g.astype(BF16)
    r1 = g - g_hi.astype(F32)
    g_mid = r1.astype(BF16)
    g_lo = (r1 - g_mid.astype(F32)).astype(BF16)
    gc = (_dot(tri, g_hi) + _dot(tri, g_mid) + _dot(tri, g_lo)) * LOG2E

    def seg_last(x):
        x3 = x.reshape(nseg, seg, W_BR)
        return jnp.broadcast_to(x3[:, seg - 1:seg, :], x3.shape).reshape(c, W_BR)

    g_end = seg_last(gc)
    q_dec = (q * jnp.exp2(gc)).astype(BF16)
    k_dec = (k * jnp.exp2(g_end - gc)).astype(BF16)
    vb = v.astype(BF16)

    def head(x, h):
        return x[:, h * HG_DK:(h + 1) * HG_DK]

    qb = q.astype(BF16)
    kb = k.astype(BF16)
    a_acc = [jnp.where(row == col, _dot_nt(head(qb, h), head(kb, h)), 0.0) for h in range(HG_HEADS)]

    hlev = 1
    while hlev < seg:
        blk = 2 * hlev
        if hlev >= SUBLANES:
            nblk = c // blk
            g3 = gc.reshape(nblk, blk, W_BR)
            ref = g3[:, hlev - 1:hlev, :]
            k_lo = k.reshape(nblk, blk, W_BR)[:, 0:hlev] * jnp.exp2(ref - g3[:, 0:hlev])
            q_up = q.reshape(nblk, blk, W_BR)[:, hlev:blk] * jnp.exp2(g3[:, hlev:blk] - ref)
            zero = jnp.zeros((nblk, hlev, W_BR), F32)
            q_lv = jnp.concatenate([zero, q_up], axis=1).reshape(c, W_BR).astype(BF16)
            k_lv = jnp.concatenate([k_lo, zero], axis=1).reshape(c, W_BR).astype(BF16)
        else:
            if hlev == 4:
                g3 = gc.reshape(c // blk, blk, W_BR)
                ref = jnp.broadcast_to(g3[:, hlev - 1:hlev, :], g3.shape).reshape(c, W_BR)
            elif hlev == 1:
                ref = jnp.where((rowi & 1) == 1, pltpu.roll(gc, 1, 0), gc)
            else:
                pos = rowi & 3
                ref = jnp.where(pos == 0, pltpu.roll(gc, c - 1, 0),
                                jnp.where(pos == 1, gc,
                                          jnp.where(pos == 2, pltpu.roll(gc, 1, 0), pltpu.roll(gc, 2, 0))))
            x = jnp.exp2(-jnp.abs(gc - ref))
            upper = (rowi & (blk - 1)) >= hlev
            q_lv = jnp.where(upper, q * x, 0.0).astype(BF16)
            k_lv = jnp.where(upper, 0.0, k * x).astype(BF16)
        shift = blk.bit_length() - 1
        same_blk = (row >> shift) == (col >> shift)
        for h in range(HG_HEADS):
            a_acc[h] = a_acc[h] + jnp.where(same_blk, _dot_nt(head(q_lv, h), head(k_lv, h)), 0.0)
        hlev = blk

    d_end = jnp.exp2(g_end)
    for h in range(HG_HEADS):
        o_h = _dot(a_acc[h].astype(BF16), head(vb, h))
        vt = head(v, h).T.astype(BF16)
        kd = head(k_dec, h)
        inter = []
        for j in range(nseg):
            st = st_ref[j, h]
            inter.append(_dot_nt(head(q_dec, h)[j * seg:(j + 1) * seg, :], st.astype(BF16)))
            kd_j = kd if nseg == 1 else jnp.where((rowi[:, :HG_DK] >> seg_shift) == j, kd, jnp.zeros_like(kd))
            dj = head(d_end, h)[j * seg:j * seg + 1, :]
            st_ref[j, h] = st * dj + _dot(vt, kd_j)
        o_h = o_h + (inter[0] if nseg == 1 else jnp.concatenate(inter, axis=0))
        o_ref[:, h * HG_DV:(h + 1) * HG_DV] = _rms(o_h, head(ng_ref[...], h))

    @pl.when(t == pl.num_programs(1) - 1)
    def _():
        for j in range(nseg):
            for h in range(HG_HEADS):
                so_ref[0, j, h] = st_ref[j, h].T


def _hgrn(qq, kk, lf, vv, s0, layer_in, norm_g, groups, steps, seg, depth, layer_out, prev_states):
    n = qq.shape[0]
    nseg = HG_TILE // seg
    tile = pl.BlockSpec((HG_TILE, W_BR), lambda gi, ti: (gi * steps + ti, 0))
    in_specs = [tile, tile, tile, tile,
                pl.BlockSpec((1, nseg, HG_HEADS, HG_DK, HG_DV), lambda gi, ti: (layer_in, gi, 0, 0, 0)),
                pl.BlockSpec((1, W_BR), lambda gi, ti: (0, 0))]
    args = [qq, kk, lf, vv, s0, norm_g]
    aliases = {}
    if prev_states is not None:
        in_specs.append(pl.BlockSpec(memory_space=pl.ANY))
        aliases = {len(args): 1}
        args.append(prev_states)
    return pl.pallas_call(
        functools.partial(_hgrn_kernel, seg=seg, aliased=prev_states is not None),
        grid=(groups, steps),
        in_specs=in_specs,
        out_specs=[tile, pl.BlockSpec((1, nseg, HG_HEADS, HG_DK, HG_DV), lambda gi, ti: (layer_out, gi, 0, 0, 0))],
        out_shape=[jax.ShapeDtypeStruct((n, W_BR), F32),
                   jax.ShapeDtypeStruct((depth, groups * nseg, HG_HEADS, HG_DK, HG_DV), F32)],
        scratch_shapes=[pltpu.VMEM((nseg, HG_HEADS, HG_DV, HG_DK), F32)],
        input_output_aliases=aliases,
        compiler_params=_params(("parallel", "arbitrary")),
        name="hgrn",
    )(*args)


def _merge_kernel(x_ref, ya_ref, yb_ref, yc_ref, g_ref, w4_ref, wbr_ref, bg_ref, wo_ref, fg_ref, o_ref, *, final):
    x = x_ref[...]
    hb = _rms(x, g_ref[...]).astype(BF16)
    gate0 = 3 * W_BR
    merged = jnp.zeros(x.shape, F32)
    for p, y_ref in enumerate((ya_ref, yb_ref, yc_ref)):
        z = _dot(hb, w4_ref[:, p * W_BR:(p + 1) * W_BR])
        ys = (y_ref[...] * (z * _sigmoid(z))).astype(BF16)
        br = _dot(ys, wbr_ref[p])
        gl = _dot(hb, w4_ref[:, gate0 + p * D_MODEL:gate0 + (p + 1) * D_MODEL]) + bg_ref[p:p + 1, :]
        merged = merged + _sigmoid(gl) * br
    out = x + _dot(merged.astype(BF16), wo_ref[...])
    if final:
        out = _rms(out, fg_ref[...])
    o_ref[...] = out


def _merge(x2d, ya, yb, yc, norm_g, w4, wbr, b_gate, wo, final_g, final, tm):
    n = x2d.shape[0]
    tm = min(tm, n)
    row = lambda i: (i, 0)
    fixed = lambda i: (0, 0)
    ysp = pl.BlockSpec((tm, W_BR), row)
    return pl.pallas_call(
        functools.partial(_merge_kernel, final=final),
        grid=(n // tm,),
        in_specs=[pl.BlockSpec((tm, D_MODEL), row), ysp, ysp, ysp, pl.BlockSpec((1, D_MODEL), fixed),
                  pl.BlockSpec(w4.shape, fixed), pl.BlockSpec(wbr.shape, lambda i: (0, 0, 0)),
                  pl.BlockSpec(b_gate.shape, fixed), pl.BlockSpec(wo.shape, fixed),
                  pl.BlockSpec((1, D_MODEL), fixed)],
        out_specs=pl.BlockSpec((tm, D_MODEL), row),
        out_shape=jax.ShapeDtypeStruct((n, D_MODEL), F32),
        compiler_params=_params(("parallel",)),
        name="merge",
    )(x2d, ya, yb, yc, norm_g, w4, wbr, b_gate, wo, final_g)


def _rope_tables(pos):
    half = ROPE_DIM // 2
    freqs = jnp.power(jnp.float32(ROPE_THETA), -jnp.arange(half, dtype=F32) / half)
    ang = pos.astype(F32)[:, None] * freqs[None, :]
    cos, sin = jnp.cos(ang), jnp.sin(ang)
    rows = pos.shape[0]
    ones = jnp.ones((rows, ATT_HD - ROPE_DIM), F32)
    zeros = jnp.zeros((rows, ATT_HD - ROPE_DIM), F32)
    zh = jnp.zeros((rows, half), F32)
    c_head = jnp.concatenate([cos, cos, ones], axis=1)
    sa_head = jnp.concatenate([-sin, zh, zeros], axis=1)
    sb_head = jnp.concatenate([zh, sin, zeros], axis=1)
    rep = LANES // ATT_HD
    return tuple(jnp.tile(x, (1, rep)) for x in (c_head, sa_head, sb_head))


def kernel(x_prompt, x_sample, cache_conv, cache_win_k, cache_win_v, state_hgrn, norm_g, w_in, conv_w, conv_b, ln_a_g, ln_a_b, hg_lb_param, hg_norm_g, w_branch, b_gate, w_out, final_norm_g):
    b, t, d = x_prompt.shape
    bs, ts, _ = x_sample.shape
    depth = w_in.shape[0]
    nrow = cache_win_k.shape[2]
    tm = 512

    lb_all = jnp.cumsum(jax.nn.softmax(hg_lb_param.astype(F32), axis=0), axis=0)
    lb_all = lb_all - lb_all[:1]

    rope_p = _rope_tables(jnp.arange(t, dtype=jnp.int32))
    rope_s = _rope_tables(PAST_LEN + jnp.arange(bs * ts, dtype=jnp.int32) % ts)
    bias = _decode_bias(nrow, ts)
    zero_state = jnp.zeros((1, b, HG_HEADS, HG_DK, HG_DV), F32)
    cache_kt = jnp.transpose(cache_win_k, (0, 1, 3, 4, 2))
    cache_vt = jnp.transpose(cache_win_v, (0, 1, 3, 4, 2))
    cache_conv_t = jnp.transpose(cache_conv, (0, 2, 1, 3))

    def head_major(a):
        a = jnp.transpose(a.reshape(bs, ts, ATT_HEADS, ATT_HD), (0, 2, 1, 3))
        return jnp.pad(a, ((0, 0), (0, 0), (0, DEC_Q - ts), (0, 0)))

    xp = x_prompt.reshape(b * t, d)
    xs = x_sample.reshape(bs * ts, d)
    conv_p, conv_s = [], []
    s_p = s_s = kvt_p = None
    kv_out = None
    seg_s = 16
    for l in range(depth):
        wl = w_in[l]
        w1 = jnp.concatenate([wl[:, 0:2 * W_BR], wl[:, 3 * W_BR:6 * W_BR], wl[:, 7 * W_BR:10 * W_BR]], axis=1).astype(BF16)
        w4 = jnp.concatenate([wl[:, 2 * W_BR:3 * W_BR], wl[:, 6 * W_BR:7 * W_BR], wl[:, 10 * W_BR:11 * W_BR],
                              wl[:, 11 * W_BR:]], axis=1).astype(BF16)
        wbr = w_branch[l].astype(BF16)
        wo = w_out[l].astype(BF16)
        lb = lb_all[l]
        lbp = jnp.stack([jnp.log(lb), jnp.log1p(-lb), 1.0 - lb])
        ng = norm_g[l][None]
        cw, cb, lg, lbias = conv_w[l], conv_b[l][None], ln_a_g[l][None], ln_a_b[l][None]
        hng = hg_norm_g[l][None]

        up, qp, kp, vp, qqp, kkp, lfp, vvp, kt_p, vt_p = _inproj(
            xp, ng, w1, *rope_p, lbp, tm, window=(t, min(WIN_MAX, t), depth, l, kvt_p))
        kvt_p = (kt_p, vt_p)
        us, qs, ks, vs, qqs, kks, lfs, vvs = _inproj(xs, ng, w1, *rope_s, lbp, tm)

        up3 = up.reshape(b, t, W_BR)
        ya_p = _conv_prompt(up3, cw, cb, lg, lbias).reshape(b * t, W_BR)
        conv_p.append(up3[:, t - (CONV_W - 1):])
        us_t = jnp.transpose(us.reshape(bs, ts, W_BR), (1, 0, 2))
        ya_st, nc = _conv_decode(cache_conv_t, l, us_t, cw, cb, lg, lbias)
        ya_s = jnp.transpose(ya_st, (1, 0, 2)).reshape(bs * ts, W_BR)
        conv_s.append(nc)

        qp3, kp3, vp3 = (a.reshape(b, t, W_BR) for a in (qp, kp, vp))
        q8, kn8, vn8 = (head_major(a) for a in (qs, ks, vs))
        q3 = jnp.pad(jnp.tile(q8, (1, 1, len(DIL_PATTERNS), 1)), ((0, 0), (0, 0), (0, DEC_SROWS - 3 * DEC_Q), (0, 0)))
        of, k_out, v_out, yb_p3 = _attn(q3, kn8, vn8, cache_kt, cache_vt, bias, l, kv_out, ts, qp3, kp3, vp3)
        kv_out = (k_out, v_out)
        yb_p = yb_p3.reshape(b * t, W_BR)
        yb_s = jnp.transpose(of[:, :, :ts], (0, 2, 1, 3)).reshape(bs * ts, W_BR)

        yc_p, s_p = _hgrn(qqp, kkp, lfp, vvp, zero_state, 0, hng, b, t // HG_TILE, HG_TILE, depth, l, s_p)

        def pad_seg(a):
            return jnp.pad(a.reshape(bs, ts, W_BR), ((0, 0), (0, seg_s - ts), (0, 0))).reshape(bs * seg_s, W_BR)

        yc_s16, s_s = _hgrn(pad_seg(qqs), pad_seg(kks), pad_seg(lfs), pad_seg(vvs), state_hgrn, l, hng,
                            bs * seg_s // HG_TILE, 1, seg_s, depth, l, s_s)
        yc_s = yc_s16.reshape(bs, seg_s, W_BR)[:, :ts].reshape(bs * ts, W_BR)

        final = l == depth - 1
        fg = final_norm_g[None]
        xp = _merge(xp, ya_p, yb_p, yc_p, ng, w4, wbr, b_gate[l], wo, fg, final, tm)
        xs = _merge(xs, ya_s, yb_s, yc_s, ng, w4, wbr, b_gate[l], wo, fg, final, tm)

    return (xp.reshape(b, t, d), xs.reshape(bs, ts, d),
            jnp.stack(conv_p), jnp.transpose(kvt_p[0], (0, 1, 4, 2, 3)), jnp.transpose(kvt_p[1], (0, 1, 4, 2, 3)), s_p,
            jnp.transpose(jnp.stack(conv_s), (0, 2, 1, 3)),
            jnp.transpose(kv_out[0], (0, 1, 4, 2, 3)), jnp.transpose(kv_out[1], (0, 1, 4, 2, 3)), s_s)
```

```python
import functools

import jax
import jax.numpy as jnp
from jax import lax
from jax.experimental import pallas as pl
from jax.experimental.pallas import tpu as pltpu

F32 = jnp.float32
BF16 = jnp.bfloat16

D_MODEL = 1024
W_BR = 512
CONV_W = 31
ATT_HD = 64
ATT_HEADS = 8
ROPE_DIM = 16
ROPE_THETA = 500000.0
DIL_PATTERNS = ((128, 1), (512, 4), (2048, 16))
N_BACK = 128
WIN_MAX = 2048
PAST_LEN = 2048
HG_DK = 128
HG_DV = 128
HG_HEADS = 4
EPS = 1e-6
NEG = -1e30
LOG2E = 1.4426950408889634

LANES = 128
SUBLANES = 8
VMEM_LIMIT = 56 * 1024 * 1024
ATT_VMEM_LIMIT = 58 * 1024 * 1024


def _params(sem, vmem_limit=VMEM_LIMIT):
    return pltpu.CompilerParams(dimension_semantics=sem, vmem_limit_bytes=vmem_limit)


def _sigmoid(x):
    return 1.0 / (1.0 + jnp.exp(-x))


def _dot(a, b):
    return jnp.dot(a, b, preferred_element_type=F32)


def _dot_nt(a, b):
    return lax.dot_general(a, b, (((1,), (1,)), ((), ())), preferred_element_type=F32)


def _rms(x, g):
    return x * lax.rsqrt(jnp.mean(x * x, axis=-1, keepdims=True) + EPS) * g


def _inproj_kernel(x_ref, g_ref, w_ref, cos_ref, sa_ref, sb_ref, lbp_ref, *rest, window_tiles, seq_tiles, aliased):
    if aliased:
        rest = rest[2:]
    u_ref, q_ref, k_ref, v_ref, qq_ref, kk_ref, lf_ref, vv_ref = rest[:8]
    hb = _rms(x_ref[...], g_ref[...]).astype(BF16)
    tm = x_ref.shape[0]

    def proj(c):
        return _dot(hb, w_ref[:, c * W_BR:(c + 1) * W_BR])

    cos = cos_ref[...]
    sa = sa_ref[...]
    sb = sb_ref[...]

    def rope(y):
        out = []
        for gi in range(W_BR // LANES):
            yg = y[:, gi * LANES:(gi + 1) * LANES]
            out.append(yg * cos + pltpu.roll(yg, LANES - ROPE_DIM // 2, 1) * sa + pltpu.roll(yg, ROPE_DIM // 2, 1) * sb)
        return out

    def store(dst_ref, groups):
        for gi, yg in enumerate(groups):
            dst_ref[:, gi * LANES:(gi + 1) * LANES] = yg

    def store_time_minor(dst_ref, groups):
        pair = LANES // ATT_HD
        for gi, yg in enumerate(groups):
            for rb in range(tm // LANES):
                blk = yg[rb * LANES:(rb + 1) * LANES, :].T
                dst_ref[0, 0, gi * pair:(gi + 1) * pair, :, rb * LANES:(rb + 1) * LANES] = blk.reshape(pair, ATT_HD, LANES)

    u_ref[...] = proj(0) * _sigmoid(proj(1))
    store(q_ref, rope(proj(2)))
    kg = rope(proj(3))
    store(k_ref, kg)
    y_v = proj(4)
    v_ref[...] = y_v
    if window_tiles:
        kt_ref, vt_ref = rest[8:10]
        store_time_minor(kt_ref, kg)
        store_time_minor(vt_ref, [y_v[:, gi * LANES:(gi + 1) * LANES] for gi in range(W_BR // LANES)])

    cq = proj(5)
    qq_ref[...] = cq * _sigmoid(cq) * (HG_DK ** -0.5)
    fp = proj(6)
    vv_ref[...] = proj(7)
    log_lb = lbp_ref[0:1, :]
    log_1m_lb = lbp_ref[1:2, :]
    one_m_lb = lbp_ref[2:3, :]
    log_sig = jnp.minimum(fp, 0.0) - jnp.log1p(jnp.exp(-jnp.abs(fp)))
    b = log_1m_lb + log_sig
    lf_ref[...] = jnp.maximum(log_lb, b) + jnp.log1p(jnp.exp(-jnp.abs(log_lb - b)))
    kk_ref[...] = one_m_lb * _sigmoid(-fp)


def _inproj(x2d, norm_g, w1, cos, sa, sb, lbp, tm, window=None):
    n = x2d.shape[0]
    tm = min(tm, n)
    nrope = cos.shape[0] // tm
    row = lambda i: (i, 0)
    fixed = lambda i: (0, 0)
    rope_spec = pl.BlockSpec((tm, LANES), lambda i: (i % nrope, 0))
    in_specs = [pl.BlockSpec((tm, D_MODEL), row), pl.BlockSpec((1, D_MODEL), fixed),
                pl.BlockSpec(w1.shape, fixed), rope_spec, rope_spec, rope_spec, pl.BlockSpec(lbp.shape, fixed)]
    args = [x2d, norm_g, w1, cos, sa, sb, lbp]
    out_specs = [pl.BlockSpec((tm, W_BR), row)] * 8
    out_shape = [jax.ShapeDtypeStruct((n, W_BR), F32)] * 8
    window_tiles = seq_tiles = 0
    aliases = {}
    if window is not None:
        seq_len, keep, depth, layer, prev = window
        seq_tiles, window_tiles = seq_len // tm, keep // tm
        first = seq_tiles - window_tiles
        wspec = pl.BlockSpec((1, 1, ATT_HEADS, ATT_HD, tm),
                             lambda i: (layer, i // seq_tiles, 0, 0, jnp.maximum(i % seq_tiles - first, 0)))
        out_specs = out_specs + [wspec, wspec]
        out_shape = out_shape + [jax.ShapeDtypeStruct((depth, n // seq_len, ATT_HEADS, ATT_HD, keep), F32)] * 2
        if prev is not None:
            in_specs += [pl.BlockSpec(memory_space=pl.ANY)] * 2
            aliases = {len(args): 8, len(args) + 1: 9}
            args += list(prev)
    return pl.pallas_call(
        functools.partial(_inproj_kernel, window_tiles=window_tiles, seq_tiles=seq_tiles, aliased=bool(aliases)),
        grid=(n // tm,),
        in_specs=in_specs,
        out_specs=out_specs,
        out_shape=out_shape,
        input_output_aliases=aliases,
        compiler_params=_params(("arbitrary",)),
        name="inproj",
    )(*args)


CONV_HALO = 32


def _ln_swish(y, lg, lb):
    mu = jnp.mean(y, axis=-1, keepdims=True)
    d = y - mu
    var = jnp.mean(d * d, axis=-1, keepdims=True)
    yn = d * lax.rsqrt(var + EPS) * lg + lb
    return yn * _sigmoid(yn)


CONV_PITCH = 2
CONV_NORM_ROWS = 128


def _conv_prompt_kernel(cur_ref, halo_ref, w_ref, b_ref, lg_ref, lb_ref, o_ref, ext_ref, cv_ref, *, tt, rc):
    i = pl.program_id(1)
    ngrp = W_BR // LANES
    nblk = rc // SUBLANES
    halo = jnp.where(i > 0, halo_ref[0], 0.0)
    off = CONV_HALO - (CONV_W - 1)
    for g in range(ngrp):
        lanes = slice(g * LANES, (g + 1) * LANES)
        ext_ref[g, pl.ds(0, CONV_HALO, stride=CONV_PITCH), :] = halo[:, lanes]
        ext_ref[g, pl.ds(CONV_PITCH * CONV_HALO, tt, stride=CONV_PITCH), :] = cur_ref[0, :, lanes]
        taps = [jnp.broadcast_to(w_ref[j:j + 1, lanes], (SUBLANES, LANES)) for j in range(CONV_W)]

        def body(c, carry, g=g, lanes=lanes, taps=taps):
            r0 = pl.multiple_of(c * rc, rc)
            acc = [None] * nblk
            for s in range(off, off + rc - SUBLANES + CONV_W):
                win = ext_ref[g, pl.ds(CONV_PITCH * (r0 + s), SUBLANES, stride=CONV_PITCH), :]
                for b in range(nblk):
                    j = s - off - SUBLANES * b
                    if 0 <= j < CONV_W:
                        term = taps[j] * win
                        acc[b] = term if acc[b] is None else acc[b] + term
            cv_ref[pl.ds(r0, rc), lanes] = jnp.concatenate(acc, axis=0)
            return carry

        lax.fori_loop(0, tt // rc, body, 0)

    def norm(c, carry):
        r0 = pl.multiple_of(c * CONV_NORM_ROWS, CONV_NORM_ROWS)
        y = cv_ref[pl.ds(r0, CONV_NORM_ROWS), :] + b_ref[...]
        o_ref[0, pl.ds(r0, CONV_NORM_ROWS), :] = _ln_swish(y, lg_ref[...], lb_ref[...])
        return carry

    lax.fori_loop(0, tt // CONV_NORM_ROWS, norm, 0)


def _conv_prompt(u, conv_w, conv_b, ln_g, ln_b, tt=512, rc=64):
    b, t, _ = u.shape
    per_tile = tt // CONV_HALO
    fixed = lambda bi, i: (0, 0)
    return pl.pallas_call(
        functools.partial(_conv_prompt_kernel, tt=tt, rc=rc),
        grid=(b, t // tt),
        in_specs=[pl.BlockSpec((1, tt, W_BR), lambda bi, i: (bi, i, 0)),
                  pl.BlockSpec((1, CONV_HALO, W_BR), lambda bi, i: (bi, jnp.maximum(i * per_tile - 1, 0), 0)),
                  pl.BlockSpec((CONV_W, W_BR), fixed), pl.BlockSpec((1, W_BR), fixed),
                  pl.BlockSpec((1, W_BR), fixed), pl.BlockSpec((1, W_BR), fixed)],
        out_specs=pl.BlockSpec((1, tt, W_BR), lambda bi, i: (bi, i, 0)),
        out_shape=jax.ShapeDtypeStruct((b, t, W_BR), F32),
        scratch_shapes=[pltpu.VMEM((W_BR // LANES, CONV_PITCH * (CONV_HALO + tt), LANES), F32),
                        pltpu.VMEM((tt, W_BR), F32)],
        compiler_params=_params(("parallel", "parallel")),
        name="conv_prompt",
    )(u, u, conv_w, conv_b, ln_g, ln_b)


def _conv_decode_kernel(cache_ref, u_ref, w_ref, b_ref, lg_ref, lb_ref, y_ref, nc_ref, *, ts):
    hist = CONV_W - 1

    def slab(r):
        return cache_ref[0, r] if r < hist else u_ref[r - hist]

    for t in range(ts):
        acc = w_ref[0:1, :] * slab(t)
        for j in range(1, CONV_W):
            acc = acc + w_ref[j:j + 1, :] * slab(t + j)
        y_ref[t] = _ln_swish(acc + b_ref[...], lg_ref[...], lb_ref[...])
    for r in range(hist):
        nc_ref[r] = slab(r + ts)


def _conv_decode(cache_t, layer, u_t, conv_w, conv_b, ln_g, ln_b, bb=32):
    ts, bs, _ = u_t.shape
    bb = min(bb, bs)
    hist = CONV_W - 1
    fixed = lambda i: (0, 0)
    return pl.pallas_call(
        functools.partial(_conv_decode_kernel, ts=ts),
        grid=(bs // bb,),
        in_specs=[pl.BlockSpec((1, hist, bb, W_BR), lambda i: (layer, 0, i, 0)),
                  pl.BlockSpec((ts, bb, W_BR), lambda i: (0, i, 0)),
                  pl.BlockSpec((CONV_W, W_BR), fixed), pl.BlockSpec((1, W_BR), fixed),
                  pl.BlockSpec((1, W_BR), fixed), pl.BlockSpec((1, W_BR), fixed)],
        out_specs=[pl.BlockSpec((ts, bb, W_BR), lambda i: (0, i, 0)),
                   pl.BlockSpec((hist, bb, W_BR), lambda i: (0, i, 0))],
        out_shape=[jax.ShapeDtypeStruct((ts, bs, W_BR), F32),
                   jax.ShapeDtypeStruct((hist, bs, W_BR), F32)],
        compiler_params=_params(("parallel",)),
        name="conv_decode",
    )(cache_t, u_t, conv_w, conv_b, ln_g, ln_b)


ATT_BLK = 128
ATT_SKEW_SM = 3
ATT_SKEW_OUT = 6


def _run_skewed(units):
    n = len(units)
    for j in range(n + ATT_SKEW_OUT):
        if j < n:
            units[j][0]()
        if 0 <= j - ATT_SKEW_SM < n:
            units[j - ATT_SKEW_SM][1]()
        if 0 <= j - ATT_SKEW_OUT < n:
            units[j - ATT_SKEW_OUT][2]()


def _interleave(major, minor):
    if not minor:
        return list(major)
    every = -(-len(major) // len(minor))
    out, rest = [], list(minor)
    for i, unit in enumerate(major):
        out.append(unit)
        if (i + 1) % every == 0 and rest:
            out.append(rest.pop(0))
    return out + rest


def _attn_prompt_units(q_ref, k_ref, v_ref, os_ref, ls_ref, it, *, t, blocks):
    trips = t // ATT_BLK // blocks
    r = lax.broadcasted_iota(jnp.int32, (ATT_BLK, ATT_BLK), 0)
    c = lax.broadcasted_iota(jnp.int32, (ATT_BLK, ATT_BLK), 1)
    own_ok = c <= r
    prev_ok = c >= r
    lane = lax.broadcasted_iota(jnp.int32, (ATT_BLK, LANES), 1)
    first_head = lane < ATT_HD
    scale = ATT_HD ** -0.5
    units = []

    for p, (_, dil) in enumerate(DIL_PATTERNS):
        nb = t // dil // ATT_BLK
        span = dil * ATT_BLK

        def rows(start, dil=dil):
            if dil == 1:
                return pl.ds(pl.multiple_of(start, ATT_BLK), ATT_BLK)
            return pl.ds(start, ATT_BLK, stride=dil)

        unroll = min(blocks, nb)
        per_cls = nb // unroll
        ncls = max(1, blocks // nb)
        assert dil // ncls * per_cls == trips

        cls0 = it // per_cls
        first = it - cls0 * per_cls
        has_prev = first > 0
        starts, cache, st = {}, {}, {}
        for g in range(ncls):
            start0 = cls0 * ncls + g + span * unroll * first
            starts[g, 0] = jnp.where(has_prev, start0 - span, start0)
            for u in range(unroll):
                starts[g, u + 1] = start0 + span * u

        def blk(ref, g, u, cache=cache, starts=starts, rows=rows):
            if (id(ref), g, u) not in cache:
                x = ref[0, rows(starts[g, u]), :].astype(BF16)
                if ref is v_ref:
                    x = jnp.concatenate([x, jnp.ones((ATT_BLK, LANES), BF16)], axis=1)
                cache[id(ref), g, u] = x
            return cache[id(ref), g, u]

        for g in range(ncls):
            for u in range(unroll):
                for hh in range(2):
                    key = (g, u, hh)

                    def scores(g=g, u=u, hh=hh, key=key, st=st, starts=starts, rows=rows, blk=blk, has_prev=has_prev):
                        if hh == 0:
                            st["q", g, u] = q_ref[0, rows(starts[g, u + 1]), :] * scale
                        head = first_head if hh == 0 else jnp.logical_not(first_head)
                        qh = jnp.where(head, st["q", g, u], 0.0).astype(BF16)
                        pmask = jnp.logical_and(prev_ok, has_prev) if u == 0 else prev_ok
                        st["so", key] = jnp.where(own_ok, _dot_nt(qh, blk(k_ref, g, u + 1)), NEG)
                        st["sp", key] = jnp.where(pmask, _dot_nt(qh, blk(k_ref, g, u)), NEG)

                    def softmax(key=key, st=st):
                        so, sp = st.pop(("so", key)), st.pop(("sp", key))
                        m = jnp.max(jnp.maximum(so, sp), axis=-1, keepdims=True)
                        st["p", key] = (jnp.exp(so - m).astype(BF16), jnp.exp(sp - m).astype(BF16))
                        st["m", key] = m

                    def output(p=p, g=g, u=u, hh=hh, key=key, st=st, starts=starts, rows=rows, blk=blk):
                        po, pp = st.pop(("p", key))
                        acc = _dot(po, blk(v_ref, g, u + 1)) + _dot(pp, blk(v_ref, g, u))
                        l = acc[:, LANES:2 * LANES]
                        st["o", key] = acc[:, 0:LANES] * (1.0 / l)
                        st["lse", key] = st.pop(("m", key)) + jnp.log(l)
                        if hh == 1:
                            dst = rows(starts[g, u + 1])
                            other = (g, u, 0)
                            os_ref[p, dst, :] = jnp.where(first_head, st.pop(("o", other)), st.pop(("o", key)))
                            ls_ref[p, dst, :] = jnp.where(first_head, st.pop(("lse", other)), st.pop(("lse", key)))

                    units.append((scores, softmax, output))
    return units


def _attn_prompt_merge(o_ref, os_ref, ls_ref, *, t, rows_c):
    def combine(i, carry):
        r0 = pl.multiple_of(i * rows_c, rows_c)
        ls = [ls_ref[p, pl.ds(r0, rows_c), :] for p in range(3)]
        mx = jnp.maximum(jnp.maximum(ls[0], ls[1]), ls[2])
        ws = [jnp.exp(x - mx) for x in ls]
        num = ws[0] * os_ref[0, pl.ds(r0, rows_c), :]
        num = num + ws[1] * os_ref[1, pl.ds(r0, rows_c), :]
        num = num + ws[2] * os_ref[2, pl.ds(r0, rows_c), :]
        o_ref[0, pl.ds(r0, rows_c), :] = num / (ws[0] + ws[1] + ws[2])
        return carry

    lax.fori_loop(0, t // rows_c, combine, 0)


DEC_Q = 8
DEC_SROWS = 32
DEC_HB = 8


def _attn_decode_units(q_ref, kn_ref, vn_ref, kc_ref, vc_ref, bias_ref, o_ref, ko_ref, vo_ref, *, ts):
    nrow = kc_ref.shape[-1]
    ext = nrow + LANES
    npat = len(DIL_PATTERNS)
    zl = jnp.zeros((DEC_Q, LANES - ATT_HD), F32)
    zr = jnp.zeros((LANES - DEC_Q, LANES), F32)

    def new_t(x):
        x = jnp.concatenate([jnp.concatenate([x, zl], axis=1), zr], axis=0)
        return x.T[0:ATT_HD, :]

    units = []
    for hh in range(DEC_HB):
        st = {}

        def scores(hh=hh, st=st):
            kext = jnp.concatenate([kc_ref[0, 0, hh], new_t(kn_ref[0, hh])], axis=1)
            st["v"] = jnp.concatenate([vc_ref[0, 0, hh], new_t(vn_ref[0, hh])], axis=1).astype(BF16)
            ko_ref[0, 0, hh] = pltpu.roll(kext, ext - ts, 1)[:, 0:nrow]
            vext = jnp.concatenate([vc_ref[0, 0, hh], new_t(vn_ref[0, hh])], axis=1)
            vo_ref[0, 0, hh] = pltpu.roll(vext, ext - ts, 1)[:, 0:nrow]
            q8 = q_ref[0, hh] * (ATT_HD ** -0.5)
            qh = jnp.concatenate([q8] * npat + [jnp.zeros((DEC_SROWS - npat * DEC_Q, ATT_HD), F32)],
                                 axis=0).astype(BF16)
            st["s"] = _dot(qh, kext.astype(BF16)) + bias_ref[...]

        def softmax(st=st):
            s = st.pop("s")
            st["m"] = jnp.max(s, axis=-1, keepdims=True)
            pexp = jnp.exp(s - st["m"])
            st["l"] = jnp.sum(pexp, axis=-1, keepdims=True)
            st["p"] = pexp.astype(BF16)

        def output(hh=hh, st=st):
            acc = _dot_nt(st.pop("p"), st.pop("v"))
            m, l = st.pop("m"), st.pop("l")
            mx = jnp.maximum(jnp.maximum(m[0:DEC_Q], m[DEC_Q:2 * DEC_Q]), m[2 * DEC_Q:3 * DEC_Q])
            num = jnp.zeros((DEC_Q, ATT_HD), F32)
            den = jnp.zeros((DEC_Q, 1), F32)
            for p in range(npat):
                w = jnp.exp(m[p * DEC_Q:(p + 1) * DEC_Q] - mx)
                num = num + w * acc[p * DEC_Q:(p + 1) * DEC_Q]
                den = den + w * l[p * DEC_Q:(p + 1) * DEC_Q]
            o_ref[0, hh] = num / den

        units.append((scores, softmax, output))
    return units


def _attn_kernel(q3_ref, kn_ref, vn_ref, kc_ref, vc_ref, bias_ref, qp_ref, kp_ref, vp_ref, *rest,
                 ts, t, rows_c, sub_steps, aliased):
    if aliased:
        rest = rest[2:]
    od_ref, ko_ref, vo_ref, op_ref, os_ref, ls_ref = rest
    sub = pl.program_id(0) % sub_steps
    prompt = _attn_prompt_units(qp_ref, kp_ref, vp_ref, os_ref, ls_ref, sub, t=t, blocks=t // ATT_BLK // sub_steps)
    decode = _attn_decode_units(q3_ref, kn_ref, vn_ref, kc_ref, vc_ref, bias_ref, od_ref, ko_ref, vo_ref, ts=ts)
    _run_skewed(_interleave(prompt, decode))

    @pl.when(sub == sub_steps - 1)
    def _():
        _attn_prompt_merge(op_ref, os_ref, ls_ref, t=t, rows_c=rows_c)


def _attn(q3, kn, vn, cache_kt, cache_vt, bias, layer, prev_out, ts, qp, kp, vp, rows_c=256):
    bs = q3.shape[0]
    nrow = cache_kt.shape[-1]
    b, t, _ = qp.shape
    ngrp = ATT_HEADS // DEC_HB
    npair = W_BR // LANES
    nsteps = bs * ngrp
    sub_steps = nsteps // (b * npair)
    assert sub_steps * b * npair == nsteps and t // ATT_BLK % sub_steps == 0
    cache = pl.BlockSpec((1, 1, DEC_HB, ATT_HD, nrow), lambda i: (layer, i // ngrp, i % ngrp, 0, 0))
    new = pl.BlockSpec((1, DEC_HB, DEC_Q, ATT_HD), lambda i: (i // ngrp, i % ngrp, 0, 0))
    prompt_map = lambda i: (i // sub_steps // npair, 0, i // sub_steps % npair)
    prompt = pl.BlockSpec((1, t, LANES), prompt_map)
    prompt_in = pl.BlockSpec((1, t, LANES), prompt_map, pipeline_mode=pl.Buffered(1))
    in_specs = [new, new, new,
                cache, cache, pl.BlockSpec(bias.shape, lambda i: (0, 0)), prompt_in, prompt_in, prompt_in]
    args = [q3, kn, vn, cache_kt, cache_vt, bias, qp, kp, vp]
    aliases = {}
    if prev_out is not None:
        in_specs += [pl.BlockSpec(memory_space=pl.ANY)] * 2
        aliases = {len(args): 1, len(args) + 1: 2}
        args += list(prev_out)
    return pl.pallas_call(
        functools.partial(_attn_kernel, ts=ts, t=t, rows_c=rows_c, sub_steps=sub_steps,
                          aliased=prev_out is not None),
        grid=(nsteps,),
        in_specs=in_specs,
        out_specs=[new, cache, cache, prompt],
        out_shape=[jax.ShapeDtypeStruct((bs, ATT_HEADS, DEC_Q, ATT_HD), F32),
                   jax.ShapeDtypeStruct(cache_kt.shape, F32), jax.ShapeDtypeStruct(cache_vt.shape, F32),
                   jax.ShapeDtypeStruct((b, t, W_BR), F32)],
        scratch_shapes=[pltpu.VMEM((3, t, LANES), F32), pltpu.VMEM((3, t, LANES), F32)],
        input_output_aliases=aliases,
        compiler_params=_params(("arbitrary",), ATT_VMEM_LIMIT),
        name="attn",
    )(*args)


def _decode_bias(nrow, ts):
    ri = jnp.arange(DEC_SROWS)
    pat, qt = ri // DEC_Q, ri % DEC_Q
    wins = jnp.array([w for w, _ in DIL_PATTERNS] + [0], jnp.int32)[pat]
    dils = jnp.array([d for _, d in DIL_PATTERNS] + [1], jnp.int32)[pat]
    real_row = (pat < len(DIL_PATTERNS)) & (qt < ts)
    ci = jnp.arange(nrow + LANES)
    key_real = ci < nrow + ts
    delta = (nrow + qt)[:, None] - ci[None, :]
    ok = (delta >= 0) & (delta % dils[:, None] == 0) & (delta <= wins[:, None]) & key_real[None, :]
    ok = ok | ~real_row[:, None]
    return jnp.where(ok, 0.0, NEG).astype(F32)


HG_TILE = 128
HG_PROMPT_TILES = 4


def _hgrn_kernel(q_ref, k_ref, g_ref, v_ref, s0_ref, ng_ref, *rest, seg, tiles, aliased):
    o_ref, so_ref, st_ref = rest[1:] if aliased else rest
    nstate = tiles * (HG_TILE // seg)
    t = pl.program_id(1)

    @pl.when(t == 0)
    def _():
        for j in range(nstate):
            for h in range(HG_HEADS):
                st_ref[j, h] = s0_ref[0, j, h].T

    live = [_hgrn_tile(q_ref, k_ref, g_ref, v_ref, ng_ref, o_ref, st_ref, n, seg=seg) for n in range(tiles)]
    while live:
        live = [gen for gen in live if next(gen, "done") != "done"]

    @pl.when(t == pl.num_programs(1) - 1)
    def _():
        for j in range(nstate):
            for h in range(HG_HEADS):
                so_ref[0, j, h] = st_ref[j, h].T


def _hgrn_tile(q_ref, k_ref, g_ref, v_ref, ng_ref, o_ref, st_ref, n, *, seg):
    c = HG_TILE
    nseg = c // seg
    q = q_ref[n]
    k = k_ref[n]
    g = g_ref[n]
    v = v_ref[n]
    row = lax.broadcasted_iota(jnp.int32, (c, c), 0)
    col = lax.broadcasted_iota(jnp.int32, (c, c), 1)
    rowi = lax.broadcasted_iota(jnp.int32, (c, W_BR), 0)
    seg_shift = seg.bit_length() - 1
    same_seg = (row >> seg_shift) == (col >> seg_shift)

    tri = jnp.where(jnp.logical_and(row >= col, same_seg), 1.0, 0.0).astype(BF16)
    g_hi = g.astype(BF16)
    r1 = g - g_hi.astype(F32)
    g_mid = r1.astype(BF16)
    g_lo = (r1 - g_mid.astype(F32)).astype(BF16)
    gc = (_dot(tri, g_hi) + _dot(tri, g_mid) + _dot(tri, g_lo)) * LOG2E

    def seg_last(x):
        x3 = x.reshape(nseg, seg, W_BR)
        return jnp.broadcast_to(x3[:, seg - 1:seg, :], x3.shape).reshape(c, W_BR)

    g_end = seg_last(gc)
    q_dec = (q * jnp.exp2(gc)).astype(BF16)
    k_dec = (k * jnp.exp2(g_end - gc)).astype(BF16)
    vb = v.astype(BF16)

    def head(x, h):
        return x[:, h * HG_DK:(h + 1) * HG_DK]

    qb = q.astype(BF16)
    kb = k.astype(BF16)
    a_acc = [jnp.where(row == col, _dot_nt(head(qb, h), head(kb, h)), 0.0) for h in range(HG_HEADS)]
    yield

    hlev = 1
    while hlev < seg:
        blk = 2 * hlev
        if hlev >= SUBLANES:
            nblk = c // blk
            g3 = gc.reshape(nblk, blk, W_BR)
            ref = g3[:, hlev - 1:hlev, :]
            k_lo = k.reshape(nblk, blk, W_BR)[:, 0:hlev] * jnp.exp2(ref - g3[:, 0:hlev])
            q_up = q.reshape(nblk, blk, W_BR)[:, hlev:blk] * jnp.exp2(g3[:, hlev:blk] - ref)
            zero = jnp.zeros((nblk, hlev, W_BR), F32)
            q_lv = jnp.concatenate([zero, q_up], axis=1).reshape(c, W_BR).astype(BF16)
            k_lv = jnp.concatenate([k_lo, zero], axis=1).reshape(c, W_BR).astype(BF16)
        else:
            if hlev == 4:
                g3 = gc.reshape(c // blk, blk, W_BR)
                ref = jnp.broadcast_to(g3[:, hlev - 1:hlev, :], g3.shape).reshape(c, W_BR)
            elif hlev == 1:
                ref = jnp.where((rowi & 1) == 1, pltpu.roll(gc, 1, 0), gc)
            else:
                pos = rowi & 3
                ref = jnp.where(pos == 0, pltpu.roll(gc, c - 1, 0),
                                jnp.where(pos == 1, gc,
                                          jnp.where(pos == 2, pltpu.roll(gc, 1, 0), pltpu.roll(gc, 2, 0))))
            x = jnp.exp2(-jnp.abs(gc - ref))
            upper = (rowi & (blk - 1)) >= hlev
            q_lv = jnp.where(upper, q * x, 0.0).astype(BF16)
            k_lv = jnp.where(upper, 0.0, k * x).astype(BF16)
        shift = blk.bit_length() - 1
        same_blk = (row >> shift) == (col >> shift)
        for h in range(HG_HEADS):
            a_acc[h] = a_acc[h] + jnp.where(same_blk, _dot_nt(head(q_lv, h), head(k_lv, h)), 0.0)
        hlev = blk
        yield

    d_end = jnp.exp2(g_end)
    for h in range(HG_HEADS):
        o_h = _dot(a_acc[h].astype(BF16), head(vb, h))
        vt = head(v, h).T.astype(BF16)
        kd = head(k_dec, h)
        inter = []
        for j in range(nseg):
            st = st_ref[n * nseg + j, h]
            inter.append(_dot_nt(head(q_dec, h)[j * seg:(j + 1) * seg, :], st.astype(BF16)))
            kd_j = kd if nseg == 1 else jnp.where((rowi[:, :HG_DK] >> seg_shift) == j, kd, jnp.zeros_like(kd))
            dj = head(d_end, h)[j * seg:j * seg + 1, :]
            st_ref[n * nseg + j, h] = st * dj + _dot(vt, kd_j)
        o_h = o_h + (inter[0] if nseg == 1 else jnp.concatenate(inter, axis=0))
        o_ref[n, :, h * HG_DV:(h + 1) * HG_DV] = _rms(o_h, head(ng_ref[...], h))
        yield


def _hgrn(qq, kk, lf, vv, s0, layer_in, norm_g, groups, steps, seg, depth, layer_out, prev_states, tiles=1):
    n = qq.shape[0]
    nseg = HG_TILE // seg
    nstate = tiles * nseg
    assert groups % tiles == 0
    tile = pl.BlockSpec((tiles, HG_TILE, W_BR), lambda gi, ti: (gi, ti, 0))
    state_in = pl.BlockSpec((1, nstate, HG_HEADS, HG_DK, HG_DV), lambda gi, ti: (layer_in, gi, 0, 0, 0))
    state_out = pl.BlockSpec((1, nstate, HG_HEADS, HG_DK, HG_DV), lambda gi, ti: (layer_out, gi, 0, 0, 0))
    in_specs = [tile, tile, tile, tile, state_in, pl.BlockSpec((1, W_BR), lambda gi, ti: (0, 0))]
    args = [a.reshape(groups, steps * HG_TILE, W_BR) for a in (qq, kk, lf, vv)] + [s0, norm_g]
    aliases = {}
    if prev_states is not None:
        in_specs.append(pl.BlockSpec(memory_space=pl.ANY))
        aliases = {len(args): 1}
        args.append(prev_states)
    o, states = pl.pallas_call(
        functools.partial(_hgrn_kernel, seg=seg, tiles=tiles, aliased=prev_states is not None),
        grid=(groups // tiles, steps),
        in_specs=in_specs,
        out_specs=[tile, state_out],
        out_shape=[jax.ShapeDtypeStruct((groups, steps * HG_TILE, W_BR), F32),
                   jax.ShapeDtypeStruct((depth, groups * nseg, HG_HEADS, HG_DK, HG_DV), F32)],
        scratch_shapes=[pltpu.VMEM((nstate, HG_HEADS, HG_DV, HG_DK), F32)],
        input_output_aliases=aliases,
        compiler_params=_params(("parallel", "arbitrary")),
        name="hgrn",
    )(*args)
    return o.reshape(n, W_BR), states


def _merge_kernel(x_ref, ya_ref, yb_ref, yc_ref, g_ref, w4_ref, wbr_ref, bg_ref, wo_ref, fg_ref, o_ref, *, final):
    x = x_ref[...]
    hb = _rms(x, g_ref[...]).astype(BF16)
    gate0 = 3 * W_BR
    merged = jnp.zeros(x.shape, F32)
    for p, y_ref in enumerate((ya_ref, yb_ref, yc_ref)):
        z = _dot(hb, w4_ref[:, p * W_BR:(p + 1) * W_BR])
        ys = (y_ref[...] * (z * _sigmoid(z))).astype(BF16)
        br = _dot(ys, wbr_ref[p])
        gl = _dot(hb, w4_ref[:, gate0 + p * D_MODEL:gate0 + (p + 1) * D_MODEL]) + bg_ref[p:p + 1, :]
        merged = merged + _sigmoid(gl) * br
    out = x + _dot(merged.astype(BF16), wo_ref[...])
    if final:
        out = _rms(out, fg_ref[...])
    o_ref[...] = out


def _merge(x2d, ya, yb, yc, norm_g, w4, wbr, b_gate, wo, final_g, final, tm):
    n = x2d.shape[0]
    tm = min(tm, n)
    row = lambda i: (i, 0)
    fixed = lambda i: (0, 0)
    ysp = pl.BlockSpec((tm, W_BR), row)
    return pl.pallas_call(
        functools.partial(_merge_kernel, final=final),
        grid=(n // tm,),
        in_specs=[pl.BlockSpec((tm, D_MODEL), row), ysp, ysp, ysp, pl.BlockSpec((1, D_MODEL), fixed),
                  pl.BlockSpec(w4.shape, fixed), pl.BlockSpec(wbr.shape, lambda i: (0, 0, 0)),
                  pl.BlockSpec(b_gate.shape, fixed), pl.BlockSpec(wo.shape, fixed),
                  pl.BlockSpec((1, D_MODEL), fixed)],
        out_specs=pl.BlockSpec((tm, D_MODEL), row),
        out_shape=jax.ShapeDtypeStruct((n, D_MODEL), F32),
        compiler_params=_params(("parallel",)),
        name="merge",
    )(x2d, ya, yb, yc, norm_g, w4, wbr, b_gate, wo, final_g)


def _rope_tables(pos):
    half = ROPE_DIM // 2
    freqs = jnp.power(jnp.float32(ROPE_THETA), -jnp.arange(half, dtype=F32) / half)
    ang = pos.astype(F32)[:, None] * freqs[None, :]
    cos, sin = jnp.cos(ang), jnp.sin(ang)
    rows = pos.shape[0]
    ones = jnp.ones((rows, ATT_HD - ROPE_DIM), F32)
    zeros = jnp.zeros((rows, ATT_HD - ROPE_DIM), F32)
    zh = jnp.zeros((rows, half), F32)
    c_head = jnp.concatenate([cos, cos, ones], axis=1)
    sa_head = jnp.concatenate([-sin, zh, zeros], axis=1)
    sb_head = jnp.concatenate([zh, sin, zeros], axis=1)
    rep = LANES // ATT_HD
    return tuple(jnp.tile(x, (1, rep)) for x in (c_head, sa_head, sb_head))


def kernel(x_prompt, x_sample, cache_conv, cache_win_k, cache_win_v, state_hgrn, norm_g, w_in, conv_w, conv_b, ln_a_g, ln_a_b, hg_lb_param, hg_norm_g, w_branch, b_gate, w_out, final_norm_g):
    b, t, d = x_prompt.shape
    bs, ts, _ = x_sample.shape
    depth = w_in.shape[0]
    nrow = cache_win_k.shape[2]
    tm = 512

    lb_all = jnp.cumsum(jax.nn.softmax(hg_lb_param.astype(F32), axis=0), axis=0)
    lb_all = lb_all - lb_all[:1]

    rope_p = _rope_tables(jnp.arange(t, dtype=jnp.int32))
    rope_s = _rope_tables(PAST_LEN + jnp.arange(bs * ts, dtype=jnp.int32) % ts)
    bias = _decode_bias(nrow, ts)
    zero_state = jnp.zeros((1, b, HG_HEADS, HG_DK, HG_DV), F32)
    cache_kt = jnp.transpose(cache_win_k, (0, 1, 3, 4, 2))
    cache_vt = jnp.transpose(cache_win_v, (0, 1, 3, 4, 2))
    cache_conv_t = jnp.transpose(cache_conv, (0, 2, 1, 3))

    def head_major(a):
        a = jnp.transpose(a.reshape(bs, ts, ATT_HEADS, ATT_HD), (0, 2, 1, 3))
        return jnp.pad(a, ((0, 0), (0, 0), (0, DEC_Q - ts), (0, 0)))

    xp = x_prompt.reshape(b * t, d)
    xs = x_sample.reshape(bs * ts, d)
    conv_p, conv_s = [], []
    s_p = s_s = kvt_p = None
    kv_out = None
    seg_s = 16
    for l in range(depth):
        wl = w_in[l]
        w1 = jnp.concatenate([wl[:, 0:2 * W_BR], wl[:, 3 * W_BR:6 * W_BR], wl[:, 7 * W_BR:10 * W_BR]], axis=1).astype(BF16)
        w4 = jnp.concatenate([wl[:, 2 * W_BR:3 * W_BR], wl[:, 6 * W_BR:7 * W_BR], wl[:, 10 * W_BR:11 * W_BR],
                              wl[:, 11 * W_BR:]], axis=1).astype(BF16)
        wbr = w_branch[l].astype(BF16)
        wo = w_out[l].astype(BF16)
        lb = lb_all[l]
        lbp = jnp.stack([jnp.log(lb), jnp.log1p(-lb), 1.0 - lb])
        ng = norm_g[l][None]
        cw, cb, lg, lbias = conv_w[l], conv_b[l][None], ln_a_g[l][None], ln_a_b[l][None]
        hng = hg_norm_g[l][None]

        up, qp, kp, vp, qqp, kkp, lfp, vvp, kt_p, vt_p = _inproj(
            xp, ng, w1, *rope_p, lbp, tm, window=(t, min(WIN_MAX, t), depth, l, kvt_p))
        kvt_p = (kt_p, vt_p)
        us, qs, ks, vs, qqs, kks, lfs, vvs = _inproj(xs, ng, w1, *rope_s, lbp, tm)

        up3 = up.reshape(b, t, W_BR)
        ya_p = _conv_prompt(up3, cw, cb, lg, lbias).reshape(b * t, W_BR)
        conv_p.append(up3[:, t - (CONV_W - 1):])
        us_t = jnp.transpose(us.reshape(bs, ts, W_BR), (1, 0, 2))
        ya_st, nc = _conv_decode(cache_conv_t, l, us_t, cw, cb, lg, lbias)
        ya_s = jnp.transpose(ya_st, (1, 0, 2)).reshape(bs * ts, W_BR)
        conv_s.append(nc)

        qp3, kp3, vp3 = (a.reshape(b, t, W_BR) for a in (qp, kp, vp))
        q8, kn8, vn8 = (head_major(a) for a in (qs, ks, vs))
        of, k_out, v_out, yb_p3 = _attn(q8, kn8, vn8, cache_kt, cache_vt, bias, l, kv_out, ts, qp3, kp3, vp3)
        kv_out = (k_out, v_out)
        yb_p = yb_p3.reshape(b * t, W_BR)
        yb_s = jnp.transpose(of[:, :, :ts], (0, 2, 1, 3)).reshape(bs * ts, W_BR)

        yc_p, s_p = _hgrn(qqp, kkp, lfp, vvp, zero_state, 0, hng, b, t // HG_TILE, HG_TILE, depth, l, s_p,
                          tiles=HG_PROMPT_TILES if b % HG_PROMPT_TILES == 0 else 1)

        def pad_seg(a):
            return jnp.pad(a.reshape(bs, ts, W_BR), ((0, 0), (0, seg_s - ts), (0, 0))).reshape(bs * seg_s, W_BR)

        yc_s16, s_s = _hgrn(pad_seg(qqs), pad_seg(kks), pad_seg(lfs), pad_seg(vvs), state_hgrn, l, hng,
                            bs * seg_s // HG_TILE, 1, seg_s, depth, l, s_s)
        yc_s = yc_s16.reshape(bs, seg_s, W_BR)[:, :ts].reshape(bs * ts, W_BR)

        final = l == depth - 1
        fg = final_norm_g[None]
        xp = _merge(xp, ya_p, yb_p, yc_p, ng, w4, wbr, b_gate[l], wo, fg, final, tm)
        xs = _merge(xs, ya_s, yb_s, yc_s, ng, w4, wbr, b_gate[l], wo, fg, final, tm)

    return (xp.reshape(b, t, d), xs.reshape(bs, ts, d),
            jnp.stack(conv_p), jnp.transpose(kvt_p[0], (0, 1, 4, 2, 3)), jnp.transpose(kvt_p[1], (0, 1, 4, 2, 3)), s_p,
            jnp.transpose(jnp.stack(conv_s), (0, 2, 1, 3)),
            jnp.transpose(kv_out[0], (0, 1, 4, 2, 3)), jnp.transpose(kv_out[1], (0, 1, 4, 2, 3)), s_s)
```

```python
import functools

import jax
import jax.numpy as jnp
from jax import lax
from jax.experimental import pallas as pl
from jax.experimental.pallas import tpu as pltpu

F32 = jnp.float32
BF16 = jnp.bfloat16

D_MODEL = 1024
W_BR = 512
CONV_W = 31
ATT_HD = 64
ATT_HEADS = 8
ROPE_DIM = 16
ROPE_THETA = 500000.0
DIL_PATTERNS = ((128, 1), (512, 4), (2048, 16))
N_BACK = 128
WIN_MAX = 2048
PAST_LEN = 2048
HG_DK = 128
HG_DV = 128
HG_HEADS = 4
EPS = 1e-6
NEG = -1e30
LOG2E = 1.4426950408889634

LANES = 128
SUBLANES = 8
VMEM_LIMIT = 56 * 1024 * 1024
ATT_VMEM_LIMIT = 58 * 1024 * 1024


def _params(sem, vmem_limit=VMEM_LIMIT):
    return pltpu.CompilerParams(dimension_semantics=sem, vmem_limit_bytes=vmem_limit)


def _sigmoid(x):
    return 1.0 / (1.0 + jnp.exp(-x))


def _dot(a, b):
    return jnp.dot(a, b, preferred_element_type=F32)


def _dot_nt(a, b):
    return lax.dot_general(a, b, (((1,), (1,)), ((), ())), preferred_element_type=F32)


def _rms(x, g):
    return x * lax.rsqrt(jnp.mean(x * x, axis=-1, keepdims=True) + EPS) * g


def _inproj_kernel(x_ref, g_ref, w_ref, cos_ref, sa_ref, sb_ref, lbp_ref, *rest, window_tiles, seq_tiles, aliased):
    if aliased:
        rest = rest[2:]
    u_ref, q_ref, k_ref, v_ref, qq_ref, kk_ref, lf_ref, vv_ref = rest[:8]
    hb = _rms(x_ref[...], g_ref[...]).astype(BF16)
    tm = x_ref.shape[0]

    def proj(c):
        return _dot(hb, w_ref[:, c * W_BR:(c + 1) * W_BR])

    cos = cos_ref[...]
    sa = sa_ref[...]
    sb = sb_ref[...]

    def rope(y):
        out = []
        for gi in range(W_BR // LANES):
            yg = y[:, gi * LANES:(gi + 1) * LANES]
            out.append(yg * cos + pltpu.roll(yg, LANES - ROPE_DIM // 2, 1) * sa + pltpu.roll(yg, ROPE_DIM // 2, 1) * sb)
        return out

    def store(dst_ref, groups):
        for gi, yg in enumerate(groups):
            dst_ref[:, gi * LANES:(gi + 1) * LANES] = yg

    def store_time_minor(dst_ref, groups):
        pair = LANES // ATT_HD
        for gi, yg in enumerate(groups):
            for rb in range(tm // LANES):
                blk = yg[rb * LANES:(rb + 1) * LANES, :].T
                dst_ref[0, 0, gi * pair:(gi + 1) * pair, :, rb * LANES:(rb + 1) * LANES] = blk.reshape(pair, ATT_HD, LANES)

    u_ref[...] = proj(0) * _sigmoid(proj(1))
    store(q_ref, rope(proj(2)))
    kg = rope(proj(3))
    store(k_ref, kg)
    y_v = proj(4)
    v_ref[...] = y_v
    if window_tiles:
        kt_ref, vt_ref = rest[8:10]
        store_time_minor(kt_ref, kg)
        store_time_minor(vt_ref, [y_v[:, gi * LANES:(gi + 1) * LANES] for gi in range(W_BR // LANES)])

    cq = proj(5)
    qq_ref[...] = cq * _sigmoid(cq) * (HG_DK ** -0.5)
    fp = proj(6)
    vv_ref[...] = proj(7)
    log_lb = lbp_ref[0:1, :]
    log_1m_lb = lbp_ref[1:2, :]
    one_m_lb = lbp_ref[2:3, :]
    log_sig = jnp.minimum(fp, 0.0) - jnp.log1p(jnp.exp(-jnp.abs(fp)))
    b = log_1m_lb + log_sig
    lf_ref[...] = jnp.maximum(log_lb, b) + jnp.log1p(jnp.exp(-jnp.abs(log_lb - b)))
    kk_ref[...] = one_m_lb * _sigmoid(-fp)


def _inproj(x2d, norm_g, w1, cos, sa, sb, lbp, tm, window=None):
    n = x2d.shape[0]
    tm = min(tm, n)
    nrope = cos.shape[0] // tm
    row = lambda i: (i, 0)
    fixed = lambda i: (0, 0)
    rope_spec = pl.BlockSpec((tm, LANES), lambda i: (i % nrope, 0))
    in_specs = [pl.BlockSpec((tm, D_MODEL), row), pl.BlockSpec((1, D_MODEL), fixed),
                pl.BlockSpec(w1.shape, fixed), rope_spec, rope_spec, rope_spec, pl.BlockSpec(lbp.shape, fixed)]
    args = [x2d, norm_g, w1, cos, sa, sb, lbp]
    out_specs = [pl.BlockSpec((tm, W_BR), row)] * 8
    out_shape = [jax.ShapeDtypeStruct((n, W_BR), F32)] * 8
    window_tiles = seq_tiles = 0
    aliases = {}
    if window is not None:
        seq_len, keep, depth, layer, prev = window
        seq_tiles, window_tiles = seq_len // tm, keep // tm
        first = seq_tiles - window_tiles
        wspec = pl.BlockSpec((1, 1, ATT_HEADS, ATT_HD, tm),
                             lambda i: (layer, i // seq_tiles, 0, 0, jnp.maximum(i % seq_tiles - first, 0)))
        out_specs = out_specs + [wspec, wspec]
        out_shape = out_shape + [jax.ShapeDtypeStruct((depth, n // seq_len, ATT_HEADS, ATT_HD, keep), F32)] * 2
        if prev is not None:
            in_specs += [pl.BlockSpec(memory_space=pl.ANY)] * 2
            aliases = {len(args): 8, len(args) + 1: 9}
            args += list(prev)
    return pl.pallas_call(
        functools.partial(_inproj_kernel, window_tiles=window_tiles, seq_tiles=seq_tiles, aliased=bool(aliases)),
        grid=(n // tm,),
        in_specs=in_specs,
        out_specs=out_specs,
        out_shape=out_shape,
        input_output_aliases=aliases,
        compiler_params=_params(("arbitrary",)),
        name="inproj",
    )(*args)


CONV_HALO = 32


def _ln_swish(y, lg, lb):
    mu = jnp.mean(y, axis=-1, keepdims=True)
    d = y - mu
    var = jnp.mean(d * d, axis=-1, keepdims=True)
    yn = d * lax.rsqrt(var + EPS) * lg + lb
    return yn * _sigmoid(yn)


CONV_PITCH = 2
CONV_NORM_ROWS = 128


def _conv_prompt_kernel(cur_ref, halo_ref, w_ref, b_ref, lg_ref, lb_ref, o_ref, ext_ref, cv_ref, *, tt, rc):
    i = pl.program_id(1)
    ngrp = W_BR // LANES
    nblk = rc // SUBLANES
    halo = jnp.where(i > 0, halo_ref[0], 0.0)
    off = CONV_HALO - (CONV_W - 1)
    for g in range(ngrp):
        lanes = slice(g * LANES, (g + 1) * LANES)
        ext_ref[g, pl.ds(0, CONV_HALO, stride=CONV_PITCH), :] = halo[:, lanes]
        ext_ref[g, pl.ds(CONV_PITCH * CONV_HALO, tt, stride=CONV_PITCH), :] = cur_ref[0, :, lanes]
        taps = [jnp.broadcast_to(w_ref[j:j + 1, lanes], (SUBLANES, LANES)) for j in range(CONV_W)]

        def body(c, carry, g=g, lanes=lanes, taps=taps):
            r0 = pl.multiple_of(c * rc, rc)
            acc = [None] * nblk
            for s in range(off, off + rc - SUBLANES + CONV_W):
                win = ext_ref[g, pl.ds(CONV_PITCH * (r0 + s), SUBLANES, stride=CONV_PITCH), :]
                for b in range(nblk):
                    j = s - off - SUBLANES * b
                    if 0 <= j < CONV_W:
                        term = taps[j] * win
                        acc[b] = term if acc[b] is None else acc[b] + term
            cv_ref[pl.ds(r0, rc), lanes] = jnp.concatenate(acc, axis=0)
            return carry

        lax.fori_loop(0, tt // rc, body, 0)

    def norm(c, carry):
        r0 = pl.multiple_of(c * CONV_NORM_ROWS, CONV_NORM_ROWS)
        y = cv_ref[pl.ds(r0, CONV_NORM_ROWS), :] + b_ref[...]
        o_ref[0, pl.ds(r0, CONV_NORM_ROWS), :] = _ln_swish(y, lg_ref[...], lb_ref[...])
        return carry

    lax.fori_loop(0, tt // CONV_NORM_ROWS, norm, 0)


def _conv_prompt(u, conv_w, conv_b, ln_g, ln_b, tt=512, rc=64):
    b, t, _ = u.shape
    per_tile = tt // CONV_HALO
    fixed = lambda bi, i: (0, 0)
    return pl.pallas_call(
        functools.partial(_conv_prompt_kernel, tt=tt, rc=rc),
        grid=(b, t // tt),
        in_specs=[pl.BlockSpec((1, tt, W_BR), lambda bi, i: (bi, i, 0)),
                  pl.BlockSpec((1, CONV_HALO, W_BR), lambda bi, i: (bi, jnp.maximum(i * per_tile - 1, 0), 0)),
                  pl.BlockSpec((CONV_W, W_BR), fixed), pl.BlockSpec((1, W_BR), fixed),
                  pl.BlockSpec((1, W_BR), fixed), pl.BlockSpec((1, W_BR), fixed)],
        out_specs=pl.BlockSpec((1, tt, W_BR), lambda bi, i: (bi, i, 0)),
        out_shape=jax.ShapeDtypeStruct((b, t, W_BR), F32),
        scratch_shapes=[pltpu.VMEM((W_BR // LANES, CONV_PITCH * (CONV_HALO + tt), LANES), F32),
                        pltpu.VMEM((tt, W_BR), F32)],
        compiler_params=_params(("parallel", "parallel")),
        name="conv_prompt",
    )(u, u, conv_w, conv_b, ln_g, ln_b)


def _conv_decode_kernel(cache_ref, u_ref, w_ref, b_ref, lg_ref, lb_ref, y_ref, nc_ref, *, ts):
    hist = CONV_W - 1

    def slab(r):
        return cache_ref[0, r] if r < hist else u_ref[r - hist]

    for t in range(ts):
        acc = w_ref[0:1, :] * slab(t)
        for j in range(1, CONV_W):
            acc = acc + w_ref[j:j + 1, :] * slab(t + j)
        y_ref[t] = _ln_swish(acc + b_ref[...], lg_ref[...], lb_ref[...])
    for r in range(hist):
        nc_ref[r] = slab(r + ts)


def _conv_decode(cache_t, layer, u_t, conv_w, conv_b, ln_g, ln_b, bb=32):
    ts, bs, _ = u_t.shape
    bb = min(bb, bs)
    hist = CONV_W - 1
    fixed = lambda i: (0, 0)
    return pl.pallas_call(
        functools.partial(_conv_decode_kernel, ts=ts),
        grid=(bs // bb,),
        in_specs=[pl.BlockSpec((1, hist, bb, W_BR), lambda i: (layer, 0, i, 0)),
                  pl.BlockSpec((ts, bb, W_BR), lambda i: (0, i, 0)),
                  pl.BlockSpec((CONV_W, W_BR), fixed), pl.BlockSpec((1, W_BR), fixed),
                  pl.BlockSpec((1, W_BR), fixed), pl.BlockSpec((1, W_BR), fixed)],
        out_specs=[pl.BlockSpec((ts, bb, W_BR), lambda i: (0, i, 0)),
                   pl.BlockSpec((hist, bb, W_BR), lambda i: (0, i, 0))],
        out_shape=[jax.ShapeDtypeStruct((ts, bs, W_BR), F32),
                   jax.ShapeDtypeStruct((hist, bs, W_BR), F32)],
        compiler_params=_params(("parallel",)),
        name="conv_decode",
    )(cache_t, u_t, conv_w, conv_b, ln_g, ln_b)


ATT_BLK = 128
ATT_SKEW_SM = 3
ATT_SKEW_OUT = 6


def _run_skewed(units):
    n = len(units)
    for j in range(n + ATT_SKEW_OUT):
        if j < n:
            units[j][0]()
        if 0 <= j - ATT_SKEW_SM < n:
            units[j - ATT_SKEW_SM][1]()
        if 0 <= j - ATT_SKEW_OUT < n:
            units[j - ATT_SKEW_OUT][2]()


def _interleave(major, minor):
    if not minor:
        return list(major)
    every = -(-len(major) // len(minor))
    out, rest = [], list(minor)
    for i, unit in enumerate(major):
        out.append(unit)
        if (i + 1) % every == 0 and rest:
            out.append(rest.pop(0))
    return out + rest


def _attn_prompt_units(q_ref, k_ref, v_ref, os_ref, ls_ref, it, *, t, blocks):
    trips = t // ATT_BLK // blocks
    r = lax.broadcasted_iota(jnp.int32, (ATT_BLK, ATT_BLK), 0)
    c = lax.broadcasted_iota(jnp.int32, (ATT_BLK, ATT_BLK), 1)
    own_ok = c <= r
    prev_ok = c >= r
    lane = lax.broadcasted_iota(jnp.int32, (ATT_BLK, LANES), 1)
    first_head = lane < ATT_HD
    scale = ATT_HD ** -0.5
    units = []

    for p, (_, dil) in enumerate(DIL_PATTERNS):
        nb = t // dil // ATT_BLK
        span = dil * ATT_BLK

        def rows(start, dil=dil):
            if dil == 1:
                return pl.ds(pl.multiple_of(start, ATT_BLK), ATT_BLK)
            return pl.ds(start, ATT_BLK, stride=dil)

        unroll = min(blocks, nb)
        per_cls = nb // unroll
        ncls = max(1, blocks // nb)
        assert dil // ncls * per_cls == trips

        cls0 = it // per_cls
        first = it - cls0 * per_cls
        has_prev = first > 0
        starts, cache, st = {}, {}, {}
        for g in range(ncls):
            start0 = cls0 * ncls + g + span * unroll * first
            starts[g, 0] = jnp.where(has_prev, start0 - span, start0)
            for u in range(unroll):
                starts[g, u + 1] = start0 + span * u

        def blk(ref, g, u, cache=cache, starts=starts, rows=rows):
            if (id(ref), g, u) not in cache:
                x = ref[0, rows(starts[g, u]), :].astype(BF16)
                if ref is v_ref:
                    x = jnp.concatenate([x, jnp.ones((ATT_BLK, LANES), BF16)], axis=1)
                cache[id(ref), g, u] = x
            return cache[id(ref), g, u]

        for g in range(ncls):
            for u in range(unroll):
                for hh in range(2):
                    key = (g, u, hh)

                    def scores(g=g, u=u, hh=hh, key=key, st=st, starts=starts, rows=rows, blk=blk, has_prev=has_prev):
                        if hh == 0:
                            st["q", g, u] = q_ref[0, rows(starts[g, u + 1]), :] * scale
                        head = first_head if hh == 0 else jnp.logical_not(first_head)
                        qh = jnp.where(head, st["q", g, u], 0.0).astype(BF16)
                        pmask = jnp.logical_and(prev_ok, has_prev) if u == 0 else prev_ok
                        st["so", key] = jnp.where(own_ok, _dot_nt(qh, blk(k_ref, g, u + 1)), NEG)
                        st["sp", key] = jnp.where(pmask, _dot_nt(qh, blk(k_ref, g, u)), NEG)

                    def softmax(key=key, st=st):
                        so, sp = st.pop(("so", key)), st.pop(("sp", key))
                        m = jnp.max(jnp.maximum(so, sp), axis=-1, keepdims=True)
                        st["p", key] = (jnp.exp(so - m).astype(BF16), jnp.exp(sp - m).astype(BF16))
                        st["m", key] = m

                    def output(p=p, g=g, u=u, hh=hh, key=key, st=st, starts=starts, rows=rows, blk=blk):
                        po, pp = st.pop(("p", key))
                        acc = _dot(po, blk(v_ref, g, u + 1)) + _dot(pp, blk(v_ref, g, u))
                        l = acc[:, LANES:2 * LANES]
                        st["o", key] = acc[:, 0:LANES] * (1.0 / l)
                        st["lse", key] = st.pop(("m", key)) + jnp.log(l)
                        if hh == 1:
                            dst = rows(starts[g, u + 1])
                            other = (g, u, 0)
                            os_ref[p, dst, :] = jnp.where(first_head, st.pop(("o", other)), st.pop(("o", key)))
                            ls_ref[p, dst, :] = jnp.where(first_head, st.pop(("lse", other)), st.pop(("lse", key)))

                    units.append((scores, softmax, output))
    return units


def _attn_prompt_merge(o_ref, os_ref, ls_ref, *, t, rows_c):
    def combine(i, carry):
        r0 = pl.multiple_of(i * rows_c, rows_c)
        ls = [ls_ref[p, pl.ds(r0, rows_c), :] for p in range(3)]
        mx = jnp.maximum(jnp.maximum(ls[0], ls[1]), ls[2])
        ws = [jnp.exp(x - mx) for x in ls]
        num = ws[0] * os_ref[0, pl.ds(r0, rows_c), :]
        num = num + ws[1] * os_ref[1, pl.ds(r0, rows_c), :]
        num = num + ws[2] * os_ref[2, pl.ds(r0, rows_c), :]
        o_ref[0, pl.ds(r0, rows_c), :] = num / (ws[0] + ws[1] + ws[2])
        return carry

    lax.fori_loop(0, t // rows_c, combine, 0)


DEC_Q = 8
DEC_SROWS = 32
DEC_HB = 8


def _attn_decode_units(q_ref, kn_ref, vn_ref, kc_ref, vc_ref, bias_ref, o_ref, ko_ref, vo_ref, *, ts):
    nrow = kc_ref.shape[-1]
    ext = nrow + LANES
    npat = len(DIL_PATTERNS)
    zl = jnp.zeros((DEC_Q, LANES - ATT_HD), F32)
    zr = jnp.zeros((LANES - DEC_Q, LANES), F32)

    def new_t(x):
        x = jnp.concatenate([jnp.concatenate([x, zl], axis=1), zr], axis=0)
        return x.T[0:ATT_HD, :]

    units = []
    for hh in range(DEC_HB):
        st = {}

        def scores(hh=hh, st=st):
            kext = jnp.concatenate([kc_ref[0, 0, hh], new_t(kn_ref[0, hh])], axis=1)
            st["v"] = jnp.concatenate([vc_ref[0, 0, hh], new_t(vn_ref[0, hh])], axis=1).astype(BF16)
            ko_ref[0, 0, hh] = pltpu.roll(kext, ext - ts, 1)[:, 0:nrow]
            vext = jnp.concatenate([vc_ref[0, 0, hh], new_t(vn_ref[0, hh])], axis=1)
            vo_ref[0, 0, hh] = pltpu.roll(vext, ext - ts, 1)[:, 0:nrow]
            q8 = q_ref[0, hh] * (ATT_HD ** -0.5)
            qh = jnp.concatenate([q8] * npat + [jnp.zeros((DEC_SROWS - npat * DEC_Q, ATT_HD), F32)],
                                 axis=0).astype(BF16)
            st["s"] = _dot(qh, kext.astype(BF16)) + bias_ref[...]

        def softmax(st=st):
            s = st.pop("s")
            st["m"] = jnp.max(s, axis=-1, keepdims=True)
            pexp = jnp.exp(s - st["m"])
            st["l"] = jnp.sum(pexp, axis=-1, keepdims=True)
            st["p"] = pexp.astype(BF16)

        def output(hh=hh, st=st):
            acc = _dot_nt(st.pop("p"), st.pop("v"))
            m, l = st.pop("m"), st.pop("l")
            mx = jnp.maximum(jnp.maximum(m[0:DEC_Q], m[DEC_Q:2 * DEC_Q]), m[2 * DEC_Q:3 * DEC_Q])
            num = jnp.zeros((DEC_Q, ATT_HD), F32)
            den = jnp.zeros((DEC_Q, 1), F32)
            for p in range(npat):
                w = jnp.exp(m[p * DEC_Q:(p + 1) * DEC_Q] - mx)
                num = num + w * acc[p * DEC_Q:(p + 1) * DEC_Q]
                den = den + w * l[p * DEC_Q:(p + 1) * DEC_Q]
            o_ref[0, hh] = num / den

        units.append((scores, softmax, output))
    return units


def _attn_kernel(q3_ref, kn_ref, vn_ref, kc_ref, vc_ref, bias_ref, qp_ref, kp_ref, vp_ref, *rest,
                 ts, t, rows_c, sub_steps, aliased):
    if aliased:
        rest = rest[2:]
    od_ref, ko_ref, vo_ref, op_ref, os_ref, ls_ref = rest
    sub = pl.program_id(0) % sub_steps
    prompt = _attn_prompt_units(qp_ref, kp_ref, vp_ref, os_ref, ls_ref, sub, t=t, blocks=t // ATT_BLK // sub_steps)
    decode = _attn_decode_units(q3_ref, kn_ref, vn_ref, kc_ref, vc_ref, bias_ref, od_ref, ko_ref, vo_ref, ts=ts)
    _run_skewed(_interleave(prompt, decode))

    @pl.when(sub == sub_steps - 1)
    def _():
        _attn_prompt_merge(op_ref, os_ref, ls_ref, t=t, rows_c=rows_c)


def _attn(q3, kn, vn, cache_kt, cache_vt, bias, layer, prev_out, ts, qp, kp, vp, rows_c=256):
    bs = q3.shape[0]
    nrow = cache_kt.shape[-1]
    b, t, _ = qp.shape
    ngrp = ATT_HEADS // DEC_HB
    npair = W_BR // LANES
    nsteps = bs * ngrp
    sub_steps = nsteps // (b * npair)
    assert sub_steps * b * npair == nsteps and t // ATT_BLK % sub_steps == 0
    cache = pl.BlockSpec((1, 1, DEC_HB, ATT_HD, nrow), lambda i: (layer, i // ngrp, i % ngrp, 0, 0))
    new = pl.BlockSpec((1, DEC_HB, DEC_Q, ATT_HD), lambda i: (i // ngrp, i % ngrp, 0, 0))
    prompt_map = lambda i: (i // sub_steps // npair, 0, i // sub_steps % npair)
    prompt = pl.BlockSpec((1, t, LANES), prompt_map)
    prompt_in = pl.BlockSpec((1, t, LANES), prompt_map, pipeline_mode=pl.Buffered(1))
    in_specs = [new, new, new,
                cache, cache, pl.BlockSpec(bias.shape, lambda i: (0, 0)), prompt_in, prompt_in, prompt_in]
    args = [q3, kn, vn, cache_kt, cache_vt, bias, qp, kp, vp]
    aliases = {}
    if prev_out is not None:
        in_specs += [pl.BlockSpec(memory_space=pl.ANY)] * 2
        aliases = {len(args): 1, len(args) + 1: 2}
        args += list(prev_out)
    return pl.pallas_call(
        functools.partial(_attn_kernel, ts=ts, t=t, rows_c=rows_c, sub_steps=sub_steps,
                          aliased=prev_out is not None),
        grid=(nsteps,),
        in_specs=in_specs,
        out_specs=[new, cache, cache, prompt],
        out_shape=[jax.ShapeDtypeStruct((bs, ATT_HEADS, DEC_Q, ATT_HD), F32),
                   jax.ShapeDtypeStruct(cache_kt.shape, F32), jax.ShapeDtypeStruct(cache_vt.shape, F32),
                   jax.ShapeDtypeStruct((b, t, W_BR), F32)],
        scratch_shapes=[pltpu.VMEM((3, t, LANES), F32), pltpu.VMEM((3, t, LANES), F32)],
        input_output_aliases=aliases,
        compiler_params=_params(("arbitrary",), ATT_VMEM_LIMIT),
        name="attn",
    )(*args)


def _decode_bias(nrow, ts):
    ri = jnp.arange(DEC_SROWS)
    pat, qt = ri // DEC_Q, ri % DEC_Q
    wins = jnp.array([w for w, _ in DIL_PATTERNS] + [0], jnp.int32)[pat]
    dils = jnp.array([d for _, d in DIL_PATTERNS] + [1], jnp.int32)[pat]
    real_row = (pat < len(DIL_PATTERNS)) & (qt < ts)
    ci = jnp.arange(nrow + LANES)
    key_real = ci < nrow + ts
    delta = (nrow + qt)[:, None] - ci[None, :]
    ok = (delta >= 0) & (delta % dils[:, None] == 0) & (delta <= wins[:, None]) & key_real[None, :]
    ok = ok | ~real_row[:, None]
    return jnp.where(ok, 0.0, NEG).astype(F32)


HG_TILE = 128
HG_PROMPT_TILES = 4


def _hgrn_kernel(q_ref, k_ref, g_ref, v_ref, s0_ref, ng_ref, *rest, seg, tiles, aliased):
    o_ref, so_ref, st_ref = rest[1:] if aliased else rest
    nstate = tiles * (HG_TILE // seg)
    t = pl.program_id(1)

    @pl.when(t == 0)
    def _():
        for j in range(nstate):
            for h in range(HG_HEADS):
                st_ref[j, h] = s0_ref[0, j, h].T

    live = [_hgrn_tile(q_ref, k_ref, g_ref, v_ref, ng_ref, o_ref, st_ref, n, seg=seg) for n in range(tiles)]
    while live:
        live = [gen for gen in live if next(gen, "done") != "done"]

    @pl.when(t == pl.num_programs(1) - 1)
    def _():
        for j in range(nstate):
            for h in range(HG_HEADS):
                so_ref[0, j, h] = st_ref[j, h].T


def _hgrn_tile(q_ref, k_ref, g_ref, v_ref, ng_ref, o_ref, st_ref, n, *, seg):
    c = HG_TILE
    nseg = c // seg
    q = q_ref[n]
    k = k_ref[n]
    g = g_ref[n]
    v = v_ref[n]
    row = lax.broadcasted_iota(jnp.int32, (c, c), 0)
    col = lax.broadcasted_iota(jnp.int32, (c, c), 1)
    rowi = lax.broadcasted_iota(jnp.int32, (c, W_BR), 0)
    seg_shift = seg.bit_length() - 1
    same_seg = (row >> seg_shift) == (col >> seg_shift)

    tri = jnp.where(jnp.logical_and(row >= col, same_seg), 1.0, 0.0).astype(BF16)
    g_hi = g.astype(BF16)
    r1 = g - g_hi.astype(F32)
    g_mid = r1.astype(BF16)
    g_lo = (r1 - g_mid.astype(F32)).astype(BF16)
    gc = (_dot(tri, g_hi) + _dot(tri, g_mid) + _dot(tri, g_lo)) * LOG2E

    def seg_last(x):
        x3 = x.reshape(nseg, seg, W_BR)
        return jnp.broadcast_to(x3[:, seg - 1:seg, :], x3.shape).reshape(c, W_BR)

    g_end = seg_last(gc)
    q_dec = (q * jnp.exp2(gc)).astype(BF16)
    k_dec = (k * jnp.exp2(g_end - gc)).astype(BF16)
    vb = v.astype(BF16)

    def head(x, h):
        return x[:, h * HG_DK:(h + 1) * HG_DK]

    qb = q.astype(BF16)
    kb = k.astype(BF16)
    a_acc = [jnp.where(row == col, _dot_nt(head(qb, h), head(kb, h)), 0.0) for h in range(HG_HEADS)]
    yield

    hlev = 1
    while hlev < seg:
        blk = 2 * hlev
        if hlev >= SUBLANES:
            nblk = c // blk
            g3 = gc.reshape(nblk, blk, W_BR)
            ref = g3[:, hlev - 1:hlev, :]
            k_lo = k.reshape(nblk, blk, W_BR)[:, 0:hlev] * jnp.exp2(ref - g3[:, 0:hlev])
            q_up = q.reshape(nblk, blk, W_BR)[:, hlev:blk] * jnp.exp2(g3[:, hlev:blk] - ref)
            zero = jnp.zeros((nblk, hlev, W_BR), F32)
            q_lv = jnp.concatenate([zero, q_up], axis=1).reshape(c, W_BR).astype(BF16)
            k_lv = jnp.concatenate([k_lo, zero], axis=1).reshape(c, W_BR).astype(BF16)
        else:
            if hlev == 4:
                g3 = gc.reshape(c // blk, blk, W_BR)
                ref = jnp.broadcast_to(g3[:, hlev - 1:hlev, :], g3.shape).reshape(c, W_BR)
            elif hlev == 1:
                ref = jnp.where((rowi & 1) == 1, pltpu.roll(gc, 1, 0), gc)
            else:
                pos = rowi & 3
                ref = jnp.where(pos == 0, pltpu.roll(gc, c - 1, 0),
                                jnp.where(pos == 1, gc,
                                          jnp.where(pos == 2, pltpu.roll(gc, 1, 0), pltpu.roll(gc, 2, 0))))
            x = jnp.exp2(-jnp.abs(gc - ref))
            upper = (rowi & (blk - 1)) >= hlev
            q_lv = jnp.where(upper, q * x, 0.0).astype(BF16)
            k_lv = jnp.where(upper, 0.0, k * x).astype(BF16)
        shift = blk.bit_length() - 1
        same_blk = (row >> shift) == (col >> shift)
        for h in range(HG_HEADS):
            a_acc[h] = a_acc[h] + jnp.where(same_blk, _dot_nt(head(q_lv, h), head(k_lv, h)), 0.0)
        hlev = blk
        yield

    d_end = jnp.exp2(g_end)
    for h in range(HG_HEADS):
        o_h = _dot(a_acc[h].astype(BF16), head(vb, h))
        vt = head(v, h).T.astype(BF16)
        kd = head(k_dec, h)
        inter = []
        for j in range(nseg):
            st = st_ref[n * nseg + j, h]
            inter.append(_dot_nt(head(q_dec, h)[j * seg:(j + 1) * seg, :], st.astype(BF16)))
            kd_j = kd if nseg == 1 else jnp.where((rowi[:, :HG_DK] >> seg_shift) == j, kd, jnp.zeros_like(kd))
            dj = head(d_end, h)[j * seg:j * seg + 1, :]
            st_ref[n * nseg + j, h] = st * dj + _dot(vt, kd_j)
        o_h = o_h + (inter[0] if nseg == 1 else jnp.concatenate(inter, axis=0))
        o_ref[n, :, h * HG_DV:(h + 1) * HG_DV] = _rms(o_h, head(ng_ref[...], h))
        yield


def _hgrn(qq, kk, lf, vv, s0, layer_in, norm_g, groups, steps, seg, depth, layer_out, prev_states, tiles=1):
    n = qq.shape[0]
    nseg = HG_TILE // seg
    nstate = tiles * nseg
    assert groups % tiles == 0
    tile = pl.BlockSpec((tiles, HG_TILE, W_BR), lambda gi, ti: (gi, ti, 0))
    state_in = pl.BlockSpec((1, nstate, HG_HEADS, HG_DK, HG_DV), lambda gi, ti: (layer_in, gi, 0, 0, 0))
    state_out = pl.BlockSpec((1, nstate, HG_HEADS, HG_DK, HG_DV), lambda gi, ti: (layer_out, gi, 0, 0, 0))
    in_specs = [tile, tile, tile, tile, state_in, pl.BlockSpec((1, W_BR), lambda gi, ti: (0, 0))]
    args = [a.reshape(groups, steps * HG_TILE, W_BR) for a in (qq, kk, lf, vv)] + [s0, norm_g]
    aliases = {}
    if prev_states is not None:
        in_specs.append(pl.BlockSpec(memory_space=pl.ANY))
        aliases = {len(args): 1}
        args.append(prev_states)
    o, states = pl.pallas_call(
        functools.partial(_hgrn_kernel, seg=seg, tiles=tiles, aliased=prev_states is not None),
        grid=(groups // tiles, steps),
        in_specs=in_specs,
        out_specs=[tile, state_out],
        out_shape=[jax.ShapeDtypeStruct((groups, steps * HG_TILE, W_BR), F32),
                   jax.ShapeDtypeStruct((depth, groups * nseg, HG_HEADS, HG_DK, HG_DV), F32)],
        scratch_shapes=[pltpu.VMEM((nstate, HG_HEADS, HG_DV, HG_DK), F32)],
        input_output_aliases=aliases,
        compiler_params=_params(("parallel", "arbitrary")),
        name="hgrn",
    )(*args)
    return o.reshape(n, W_BR), states


def _merge_kernel(x_ref, ya_ref, yb_ref, yc_ref, g_ref, w4_ref, wbr_ref, bg_ref, wo_ref, fg_ref, o_ref, *, final):
    x = x_ref[...]
    hb = _rms(x, g_ref[...]).astype(BF16)
    gate0 = 3 * W_BR
    merged = jnp.zeros(x.shape, F32)
    for p, y_ref in enumerate((ya_ref, yb_ref, yc_ref)):
        z = _dot(hb, w4_ref[:, p * W_BR:(p + 1) * W_BR])
        ys = (y_ref[...] * (z * _sigmoid(z))).astype(BF16)
        br = _dot(ys, wbr_ref[p])
        gl = _dot(hb, w4_ref[:, gate0 + p * D_MODEL:gate0 + (p + 1) * D_MODEL]) + bg_ref[p:p + 1, :]
        merged = merged + _sigmoid(gl) * br
    out = x + _dot(merged.astype(BF16), wo_ref[...])
    if final:
        out = _rms(out, fg_ref[...])
    o_ref[...] = out


def _merge(x2d, ya, yb, yc, norm_g, w4, wbr, b_gate, wo, final_g, final, tm):
    n = x2d.shape[0]
    tm = min(tm, n)
    row = lambda i: (i, 0)
    fixed = lambda i: (0, 0)
    ysp = pl.BlockSpec((tm, W_BR), row)
    return pl.pallas_call(
        functools.partial(_merge_kernel, final=final),
        grid=(n // tm,),
        in_specs=[pl.BlockSpec((tm, D_MODEL), row), ysp, ysp, ysp, pl.BlockSpec((1, D_MODEL), fixed),
                  pl.BlockSpec(w4.shape, fixed, pipeline_mode=pl.Buffered(1)),
                  pl.BlockSpec(wbr.shape, lambda i: (0, 0, 0), pipeline_mode=pl.Buffered(1)),
                  pl.BlockSpec(b_gate.shape, fixed), pl.BlockSpec(wo.shape, fixed, pipeline_mode=pl.Buffered(1)),
                  pl.BlockSpec((1, D_MODEL), fixed)],
        out_specs=pl.BlockSpec((tm, D_MODEL), row),
        out_shape=jax.ShapeDtypeStruct((n, D_MODEL), F32),
        compiler_params=_params(("parallel",)),
        name="merge",
    )(x2d, ya, yb, yc, norm_g, w4, wbr, b_gate, wo, final_g)


def _rope_tables(pos):
    half = ROPE_DIM // 2
    freqs = jnp.power(jnp.float32(ROPE_THETA), -jnp.arange(half, dtype=F32) / half)
    ang = pos.astype(F32)[:, None] * freqs[None, :]
    cos, sin = jnp.cos(ang), jnp.sin(ang)
    rows = pos.shape[0]
    ones = jnp.ones((rows, ATT_HD - ROPE_DIM), F32)
    zeros = jnp.zeros((rows, ATT_HD - ROPE_DIM), F32)
    zh = jnp.zeros((rows, half), F32)
    c_head = jnp.concatenate([cos, cos, ones], axis=1)
    sa_head = jnp.concatenate([-sin, zh, zeros], axis=1)
    sb_head = jnp.concatenate([zh, sin, zeros], axis=1)
    rep = LANES // ATT_HD
    return tuple(jnp.tile(x, (1, rep)) for x in (c_head, sa_head, sb_head))


def kernel(x_prompt, x_sample, cache_conv, cache_win_k, cache_win_v, state_hgrn, norm_g, w_in, conv_w, conv_b, ln_a_g, ln_a_b, hg_lb_param, hg_norm_g, w_branch, b_gate, w_out, final_norm_g):
    b, t, d = x_prompt.shape
    bs, ts, _ = x_sample.shape
    depth = w_in.shape[0]
    nrow = cache_win_k.shape[2]
    tm = 512

    lb_all = jnp.cumsum(jax.nn.softmax(hg_lb_param.astype(F32), axis=0), axis=0)
    lb_all = lb_all - lb_all[:1]

    rope_p = _rope_tables(jnp.arange(t, dtype=jnp.int32))
    rope_s = _rope_tables(PAST_LEN + jnp.arange(bs * ts, dtype=jnp.int32) % ts)
    bias = _decode_bias(nrow, ts)
    zero_state = jnp.zeros((1, b, HG_HEADS, HG_DK, HG_DV), F32)
    cache_kt = jnp.transpose(cache_win_k, (0, 1, 3, 4, 2))
    cache_vt = jnp.transpose(cache_win_v, (0, 1, 3, 4, 2))
    cache_conv_t = jnp.transpose(cache_conv, (0, 2, 1, 3))

    def head_major(a):
        a = jnp.transpose(a.reshape(bs, ts, ATT_HEADS, ATT_HD), (0, 2, 1, 3))
        return jnp.pad(a, ((0, 0), (0, 0), (0, DEC_Q - ts), (0, 0)))

    xp = x_prompt.reshape(b * t, d)
    xs = x_sample.reshape(bs * ts, d)
    conv_p, conv_s = [], []
    s_p = s_s = kvt_p = None
    kv_out = None
    seg_s = 16
    for l in range(depth):
        wl = w_in[l]
        w1 = jnp.concatenate([wl[:, 0:2 * W_BR], wl[:, 3 * W_BR:6 * W_BR], wl[:, 7 * W_BR:10 * W_BR]], axis=1).astype(BF16)
        w4 = jnp.concatenate([wl[:, 2 * W_BR:3 * W_BR], wl[:, 6 * W_BR:7 * W_BR], wl[:, 10 * W_BR:11 * W_BR],
                              wl[:, 11 * W_BR:]], axis=1).astype(BF16)
        wbr = w_branch[l].astype(BF16)
        wo = w_out[l].astype(BF16)
        lb = lb_all[l]
        lbp = jnp.stack([jnp.log(lb), jnp.log1p(-lb), 1.0 - lb])
        ng = norm_g[l][None]
        cw, cb, lg, lbias = conv_w[l], conv_b[l][None], ln_a_g[l][None], ln_a_b[l][None]
        hng = hg_norm_g[l][None]

        up, qp, kp, vp, qqp, kkp, lfp, vvp, kt_p, vt_p = _inproj(
            xp, ng, w1, *rope_p, lbp, tm, window=(t, min(WIN_MAX, t), depth, l, kvt_p))
        kvt_p = (kt_p, vt_p)
        us, qs, ks, vs, qqs, kks, lfs, vvs = _inproj(xs, ng, w1, *rope_s, lbp, tm)

        up3 = up.reshape(b, t, W_BR)
        ya_p = _conv_prompt(up3, cw, cb, lg, lbias).reshape(b * t, W_BR)
        conv_p.append(up3[:, t - (CONV_W - 1):])
        us_t = jnp.transpose(us.reshape(bs, ts, W_BR), (1, 0, 2))
        ya_st, nc = _conv_decode(cache_conv_t, l, us_t, cw, cb, lg, lbias)
        ya_s = jnp.transpose(ya_st, (1, 0, 2)).reshape(bs * ts, W_BR)
        conv_s.append(nc)

        qp3, kp3, vp3 = (a.reshape(b, t, W_BR) for a in (qp, kp, vp))
        q8, kn8, vn8 = (head_major(a) for a in (qs, ks, vs))
        of, k_out, v_out, yb_p3 = _attn(q8, kn8, vn8, cache_kt, cache_vt, bias, l, kv_out, ts, qp3, kp3, vp3)
        kv_out = (k_out, v_out)
        yb_p = yb_p3.reshape(b * t, W_BR)
        yb_s = jnp.transpose(of[:, :, :ts], (0, 2, 1, 3)).reshape(bs * ts, W_BR)

        yc_p, s_p = _hgrn(qqp, kkp, lfp, vvp, zero_state, 0, hng, b, t // HG_TILE, HG_TILE, depth, l, s_p,
                          tiles=HG_PROMPT_TILES if b % HG_PROMPT_TILES == 0 else 1)

        def pad_seg(a):
            return jnp.pad(a.reshape(bs, ts, W_BR), ((0, 0), (0, seg_s - ts), (0, 0))).reshape(bs * seg_s, W_BR)

        yc_s16, s_s = _hgrn(pad_seg(qqs), pad_seg(kks), pad_seg(lfs), pad_seg(vvs), state_hgrn, l, hng,
                            bs * seg_s // HG_TILE, 1, seg_s, depth, l, s_s)
        yc_s = yc_s16.reshape(bs, seg_s, W_BR)[:, :ts].reshape(bs * ts, W_BR)

        final = l == depth - 1
        fg = final_norm_g[None]
        xp = _merge(xp, ya_p, yb_p, yc_p, ng, w4, wbr, b_gate[l], wo, fg, final, 2 * tm)
        xs = _merge(xs, ya_s, yb_s, yc_s, ng, w4, wbr, b_gate[l], wo, fg, final, tm)

    return (xp.reshape(b, t, d), xs.reshape(bs, ts, d),
            jnp.stack(conv_p), jnp.transpose(kvt_p[0], (0, 1, 4, 2, 3)), jnp.transpose(kvt_p[1], (0, 1, 4, 2, 3)), s_p,
            jnp.transpose(jnp.stack(conv_s), (0, 2, 1, 3)),
            jnp.transpose(kv_out[0], (0, 1, 4, 2, 3)), jnp.transpose(kv_out[1], (0, 1, 4, 2, 3)), s_s)
```

```python
import functools

import jax
import jax.numpy as jnp
from jax import lax
from jax.experimental import pallas as pl
from jax.experimental.pallas import tpu as pltpu

F32 = jnp.float32
BF16 = jnp.bfloat16

D_MODEL = 1024
W_BR = 512
CONV_W = 31
ATT_HD = 64
ATT_HEADS = 8
ROPE_DIM = 16
ROPE_THETA = 500000.0
DIL_PATTERNS = ((128, 1), (512, 4), (2048, 16))
N_BACK = 128
WIN_MAX = 2048
PAST_LEN = 2048
HG_DK = 128
HG_DV = 128
HG_HEADS = 4
EPS = 1e-6
NEG = -1e30
LOG2E = 1.4426950408889634

LANES = 128
SUBLANES = 8
VMEM_LIMIT = 56 * 1024 * 1024
ATT_VMEM_LIMIT = 58 * 1024 * 1024


def _params(sem, vmem_limit=VMEM_LIMIT):
    return pltpu.CompilerParams(dimension_semantics=sem, vmem_limit_bytes=vmem_limit)


def _sigmoid(x):
    return 1.0 / (1.0 + jnp.exp(-x))


def _dot(a, b):
    return jnp.dot(a, b, preferred_element_type=F32)


def _dot_nt(a, b):
    return lax.dot_general(a, b, (((1,), (1,)), ((), ())), preferred_element_type=F32)


def _rms(x, g):
    return x * lax.rsqrt(jnp.mean(x * x, axis=-1, keepdims=True) + EPS) * g


def _inproj_kernel(x_ref, g_ref, w_ref, cos_ref, sa_ref, sb_ref, lbp_ref, *rest, window_tiles, seq_tiles, aliased):
    if aliased:
        rest = rest[2:]
    u_ref, q_ref, k_ref, v_ref, qq_ref, kk_ref, lf_ref, vv_ref = rest[:8]
    hb = _rms(x_ref[...], g_ref[...]).astype(BF16)
    tm = x_ref.shape[0]

    def proj(c):
        return _dot(hb, w_ref[:, c * W_BR:(c + 1) * W_BR])

    cos = cos_ref[...]
    sa = sa_ref[...]
    sb = sb_ref[...]

    def rope(y):
        out = []
        for gi in range(W_BR // LANES):
            yg = y[:, gi * LANES:(gi + 1) * LANES]
            out.append(yg * cos + pltpu.roll(yg, LANES - ROPE_DIM // 2, 1) * sa + pltpu.roll(yg, ROPE_DIM // 2, 1) * sb)
        return out

    def store(dst_ref, groups):
        for gi, yg in enumerate(groups):
            dst_ref[:, gi * LANES:(gi + 1) * LANES] = yg

    def store_time_minor(dst_ref, groups):
        pair = LANES // ATT_HD
        for gi, yg in enumerate(groups):
            for rb in range(tm // LANES):
                blk = yg[rb * LANES:(rb + 1) * LANES, :].T
                dst_ref[0, 0, gi * pair:(gi + 1) * pair, :, rb * LANES:(rb + 1) * LANES] = blk.reshape(pair, ATT_HD, LANES)

    u_ref[...] = proj(0) * _sigmoid(proj(1))
    store(q_ref, rope(proj(2)))
    kg = rope(proj(3))
    store(k_ref, kg)
    y_v = proj(4)
    v_ref[...] = y_v
    if window_tiles:
        kt_ref, vt_ref = rest[8:10]
        store_time_minor(kt_ref, kg)
        store_time_minor(vt_ref, [y_v[:, gi * LANES:(gi + 1) * LANES] for gi in range(W_BR // LANES)])

    cq = proj(5)
    qq_ref[...] = cq * _sigmoid(cq) * (HG_DK ** -0.5)
    fp = proj(6)
    vv_ref[...] = proj(7)
    log_lb = lbp_ref[0:1, :]
    log_1m_lb = lbp_ref[1:2, :]
    one_m_lb = lbp_ref[2:3, :]
    log_sig = jnp.minimum(fp, 0.0) - jnp.log1p(jnp.exp(-jnp.abs(fp)))
    b = log_1m_lb + log_sig
    lf_ref[...] = jnp.maximum(log_lb, b) + jnp.log1p(jnp.exp(-jnp.abs(log_lb - b)))
    kk_ref[...] = one_m_lb * _sigmoid(-fp)


def _inproj(x2d, norm_g, w1, cos, sa, sb, lbp, tm, window=None):
    n = x2d.shape[0]
    tm = min(tm, n)
    nrope = cos.shape[0] // tm
    row = lambda i: (i, 0)
    fixed = lambda i: (0, 0)
    rope_spec = pl.BlockSpec((tm, LANES), lambda i: (i % nrope, 0))
    in_specs = [pl.BlockSpec((tm, D_MODEL), row), pl.BlockSpec((1, D_MODEL), fixed),
                pl.BlockSpec(w1.shape, fixed), rope_spec, rope_spec, rope_spec, pl.BlockSpec(lbp.shape, fixed)]
    args = [x2d, norm_g, w1, cos, sa, sb, lbp]
    out_specs = [pl.BlockSpec((tm, W_BR), row)] * 8
    out_shape = [jax.ShapeDtypeStruct((n, W_BR), F32)] * 8
    window_tiles = seq_tiles = 0
    aliases = {}
    if window is not None:
        seq_len, keep, depth, layer, prev = window
        seq_tiles, window_tiles = seq_len // tm, keep // tm
        first = seq_tiles - window_tiles
        wspec = pl.BlockSpec((1, 1, ATT_HEADS, ATT_HD, tm),
                             lambda i: (layer, i // seq_tiles, 0, 0, jnp.maximum(i % seq_tiles - first, 0)))
        out_specs = out_specs + [wspec, wspec]
        out_shape = out_shape + [jax.ShapeDtypeStruct((depth, n // seq_len, ATT_HEADS, ATT_HD, keep), F32)] * 2
        if prev is not None:
            in_specs += [pl.BlockSpec(memory_space=pl.ANY)] * 2
            aliases = {len(args): 8, len(args) + 1: 9}
            args += list(prev)
    return pl.pallas_call(
        functools.partial(_inproj_kernel, window_tiles=window_tiles, seq_tiles=seq_tiles, aliased=bool(aliases)),
        grid=(n // tm,),
        in_specs=in_specs,
        out_specs=out_specs,
        out_shape=out_shape,
        input_output_aliases=aliases,
        compiler_params=_params(("arbitrary",)),
        name="inproj",
    )(*args)


CONV_HALO = 32


def _ln_swish(y, lg, lb):
    mu = jnp.mean(y, axis=-1, keepdims=True)
    d = y - mu
    var = jnp.mean(d * d, axis=-1, keepdims=True)
    yn = d * lax.rsqrt(var + EPS) * lg + lb
    return yn * _sigmoid(yn)


CONV_PITCH = 2
CONV_NORM_ROWS = 256


def _conv_prompt_kernel(cur_ref, halo_ref, w_ref, b_ref, lg_ref, lb_ref, o_ref, ext_ref, cv_ref, *, tt, rc):
    i = pl.program_id(1)
    ngrp = W_BR // LANES
    nblk = rc // SUBLANES
    halo = jnp.where(i > 0, halo_ref[0], 0.0)
    off = CONV_HALO - (CONV_W - 1)
    for g in range(ngrp):
        lanes = slice(g * LANES, (g + 1) * LANES)
        ext_ref[g, pl.ds(0, CONV_HALO, stride=CONV_PITCH), :] = halo[:, lanes]
        ext_ref[g, pl.ds(CONV_PITCH * CONV_HALO, tt, stride=CONV_PITCH), :] = cur_ref[0, :, lanes]
        taps = [jnp.broadcast_to(w_ref[j:j + 1, lanes], (SUBLANES, LANES)) for j in range(CONV_W)]

        def body(c, carry, g=g, lanes=lanes, taps=taps):
            r0 = pl.multiple_of(c * rc, rc)
            acc = [None] * nblk
            for s in range(off, off + rc - SUBLANES + CONV_W):
                win = ext_ref[g, pl.ds(CONV_PITCH * (r0 + s), SUBLANES, stride=CONV_PITCH), :]
                for b in range(nblk):
                    j = s - off - SUBLANES * b
                    if 0 <= j < CONV_W:
                        term = taps[j] * win
                        acc[b] = term if acc[b] is None else acc[b] + term
            cv_ref[pl.ds(r0, rc), lanes] = jnp.concatenate(acc, axis=0)
            return carry

        lax.fori_loop(0, tt // rc, body, 0)

    def norm(c, carry):
        r0 = pl.multiple_of(c * CONV_NORM_ROWS, CONV_NORM_ROWS)
        y = cv_ref[pl.ds(r0, CONV_NORM_ROWS), :] + b_ref[...]
        o_ref[0, pl.ds(r0, CONV_NORM_ROWS), :] = _ln_swish(y, lg_ref[...], lb_ref[...])
        return carry

    lax.fori_loop(0, tt // CONV_NORM_ROWS, norm, 0)


def _conv_prompt(u, conv_w, conv_b, ln_g, ln_b, tt=512, rc=64):
    b, t, _ = u.shape
    per_tile = tt // CONV_HALO
    fixed = lambda bi, i: (0, 0)
    return pl.pallas_call(
        functools.partial(_conv_prompt_kernel, tt=tt, rc=rc),
        grid=(b, t // tt),
        in_specs=[pl.BlockSpec((1, tt, W_BR), lambda bi, i: (bi, i, 0)),
                  pl.BlockSpec((1, CONV_HALO, W_BR), lambda bi, i: (bi, jnp.maximum(i * per_tile - 1, 0), 0)),
                  pl.BlockSpec((CONV_W, W_BR), fixed), pl.BlockSpec((1, W_BR), fixed),
                  pl.BlockSpec((1, W_BR), fixed), pl.BlockSpec((1, W_BR), fixed)],
        out_specs=pl.BlockSpec((1, tt, W_BR), lambda bi, i: (bi, i, 0)),
        out_shape=jax.ShapeDtypeStruct((b, t, W_BR), F32),
        scratch_shapes=[pltpu.VMEM((W_BR // LANES, CONV_PITCH * (CONV_HALO + tt), LANES), F32),
                        pltpu.VMEM((tt, W_BR), F32)],
        compiler_params=_params(("parallel", "parallel")),
        name="conv_prompt",
    )(u, u, conv_w, conv_b, ln_g, ln_b)


def _conv_decode_kernel(cache_ref, u_ref, w_ref, b_ref, lg_ref, lb_ref, y_ref, nc_ref, *, ts):
    hist = CONV_W - 1

    def slab(r):
        return cache_ref[0, r] if r < hist else u_ref[r - hist]

    for t in range(ts):
        acc = w_ref[0:1, :] * slab(t)
        for j in range(1, CONV_W):
            acc = acc + w_ref[j:j + 1, :] * slab(t + j)
        y_ref[t] = _ln_swish(acc + b_ref[...], lg_ref[...], lb_ref[...])
    for r in range(hist):
        nc_ref[r] = slab(r + ts)


def _conv_decode(cache_t, layer, u_t, conv_w, conv_b, ln_g, ln_b, bb=32):
    ts, bs, _ = u_t.shape
    bb = min(bb, bs)
    hist = CONV_W - 1
    fixed = lambda i: (0, 0)
    return pl.pallas_call(
        functools.partial(_conv_decode_kernel, ts=ts),
        grid=(bs // bb,),
        in_specs=[pl.BlockSpec((1, hist, bb, W_BR), lambda i: (layer, 0, i, 0)),
                  pl.BlockSpec((ts, bb, W_BR), lambda i: (0, i, 0)),
                  pl.BlockSpec((CONV_W, W_BR), fixed), pl.BlockSpec((1, W_BR), fixed),
                  pl.BlockSpec((1, W_BR), fixed), pl.BlockSpec((1, W_BR), fixed)],
        out_specs=[pl.BlockSpec((ts, bb, W_BR), lambda i: (0, i, 0)),
                   pl.BlockSpec((hist, bb, W_BR), lambda i: (0, i, 0))],
        out_shape=[jax.ShapeDtypeStruct((ts, bs, W_BR), F32),
                   jax.ShapeDtypeStruct((hist, bs, W_BR), F32)],
        compiler_params=_params(("parallel",)),
        name="conv_decode",
    )(cache_t, u_t, conv_w, conv_b, ln_g, ln_b)


ATT_BLK = 128
ATT_SKEW_SM = 2
ATT_SKEW_OUT = 4


def _run_skewed(units):
    n = len(units)
    for j in range(n + ATT_SKEW_OUT):
        if j < n:
            units[j][0]()
        if 0 <= j - ATT_SKEW_SM < n:
            units[j - ATT_SKEW_SM][1]()
        if 0 <= j - ATT_SKEW_OUT < n:
            units[j - ATT_SKEW_OUT][2]()


def _interleave(major, minor):
    if not minor:
        return list(major)
    every = -(-len(major) // len(minor))
    out, rest = [], list(minor)
    for i, unit in enumerate(major):
        out.append(unit)
        if (i + 1) % every == 0 and rest:
            out.append(rest.pop(0))
    return out + rest


def _attn_prompt_units(q_ref, k_ref, v_ref, os_ref, ls_ref, it, *, t, blocks):
    trips = t // ATT_BLK // blocks
    r = lax.broadcasted_iota(jnp.int32, (ATT_BLK, ATT_BLK), 0)
    c = lax.broadcasted_iota(jnp.int32, (ATT_BLK, ATT_BLK), 1)
    own_ok = c <= r
    prev_ok = c >= r
    lane = lax.broadcasted_iota(jnp.int32, (ATT_BLK, LANES), 1)
    first_head = lane < ATT_HD
    scale = ATT_HD ** -0.5
    units = []

    for p, (_, dil) in enumerate(DIL_PATTERNS):
        nb = t // dil // ATT_BLK
        span = dil * ATT_BLK

        def rows(start, dil=dil):
            if dil == 1:
                return pl.ds(pl.multiple_of(start, ATT_BLK), ATT_BLK)
            return pl.ds(start, ATT_BLK, stride=dil)

        unroll = min(blocks, nb)
        per_cls = nb // unroll
        ncls = max(1, blocks // nb)
        assert dil // ncls * per_cls == trips

        cls0 = it // per_cls
        first = it - cls0 * per_cls
        has_prev = first > 0
        starts, cache, st = {}, {}, {}
        for g in range(ncls):
            start0 = cls0 * ncls + g + span * unroll * first
            starts[g, 0] = jnp.where(has_prev, start0 - span, start0)
            for u in range(unroll):
                starts[g, u + 1] = start0 + span * u

        def blk(ref, g, u, cache=cache, starts=starts, rows=rows):
            if (id(ref), g, u) not in cache:
                x = ref[0, rows(starts[g, u]), :].astype(BF16)
                if ref is v_ref:
                    x = jnp.concatenate([x, jnp.ones((ATT_BLK, LANES), BF16)], axis=1)
                cache[id(ref), g, u] = x
            return cache[id(ref), g, u]

        for g in range(ncls):
            for u in range(unroll):
                for hh in range(2):
                    key = (g, u, hh)

                    def scores(g=g, u=u, hh=hh, key=key, st=st, starts=starts, rows=rows, blk=blk, has_prev=has_prev):
                        if hh == 0:
                            st["q", g, u] = q_ref[0, rows(starts[g, u + 1]), :] * scale
                        head = first_head if hh == 0 else jnp.logical_not(first_head)
                        qh = jnp.where(head, st["q", g, u], 0.0).astype(BF16)
                        pmask = jnp.logical_and(prev_ok, has_prev) if u == 0 else prev_ok
                        st["so", key] = jnp.where(own_ok, _dot_nt(qh, blk(k_ref, g, u + 1)), NEG)
                        st["sp", key] = jnp.where(pmask, _dot_nt(qh, blk(k_ref, g, u)), NEG)

                    def softmax(key=key, st=st):
                        so, sp = st.pop(("so", key)), st.pop(("sp", key))
                        m = jnp.max(jnp.maximum(so, sp), axis=-1, keepdims=True)
                        st["p", key] = (jnp.exp(so - m).astype(BF16), jnp.exp(sp - m).astype(BF16))
                        st["m", key] = m

                    def output(p=p, g=g, u=u, hh=hh, key=key, st=st, starts=starts, rows=rows, blk=blk):
                        po, pp = st.pop(("p", key))
                        acc = _dot(po, blk(v_ref, g, u + 1)) + _dot(pp, blk(v_ref, g, u))
                        l = acc[:, LANES:2 * LANES]
                        st["o", key] = acc[:, 0:LANES] * (1.0 / l)
                        st["lse", key] = st.pop(("m", key)) + jnp.log(l)
                        if hh == 1:
                            dst = rows(starts[g, u + 1])
                            other = (g, u, 0)
                            os_ref[p, dst, :] = jnp.where(first_head, st.pop(("o", other)), st.pop(("o", key)))
                            ls_ref[p, dst, :] = jnp.where(first_head, st.pop(("lse", other)), st.pop(("lse", key)))

                    units.append((scores, softmax, output))
    return units


def _attn_prompt_merge(o_ref, os_ref, ls_ref, *, t, rows_c):
    def combine(i, carry):
        r0 = pl.multiple_of(i * rows_c, rows_c)
        ls = [ls_ref[p, pl.ds(r0, rows_c), :] for p in range(3)]
        mx = jnp.maximum(jnp.maximum(ls[0], ls[1]), ls[2])
        ws = [jnp.exp(x - mx) for x in ls]
        num = ws[0] * os_ref[0, pl.ds(r0, rows_c), :]
        num = num + ws[1] * os_ref[1, pl.ds(r0, rows_c), :]
        num = num + ws[2] * os_ref[2, pl.ds(r0, rows_c), :]
        o_ref[0, pl.ds(r0, rows_c), :] = num / (ws[0] + ws[1] + ws[2])
        return carry

    lax.fori_loop(0, t // rows_c, combine, 0)


DEC_Q = 8
DEC_SROWS = 32
DEC_HB = 8


def _attn_decode_units(q_ref, kn_ref, vn_ref, kc_ref, vc_ref, bias_ref, o_ref, ko_ref, vo_ref, *, ts):
    nrow = kc_ref.shape[-1]
    ext = nrow + LANES
    npat = len(DIL_PATTERNS)
    zl = jnp.zeros((DEC_Q, LANES - ATT_HD), F32)
    zr = jnp.zeros((LANES - DEC_Q, LANES), F32)

    def new_t(x):
        x = jnp.concatenate([jnp.concatenate([x, zl], axis=1), zr], axis=0)
        return x.T[0:ATT_HD, :]

    units = []
    for hh in range(DEC_HB):
        st = {}

        def scores(hh=hh, st=st):
            kext = jnp.concatenate([kc_ref[0, 0, hh], new_t(kn_ref[0, hh])], axis=1)
            st["v"] = jnp.concatenate([vc_ref[0, 0, hh], new_t(vn_ref[0, hh])], axis=1).astype(BF16)
            ko_ref[0, 0, hh] = pltpu.roll(kext, ext - ts, 1)[:, 0:nrow]
            vext = jnp.concatenate([vc_ref[0, 0, hh], new_t(vn_ref[0, hh])], axis=1)
            vo_ref[0, 0, hh] = pltpu.roll(vext, ext - ts, 1)[:, 0:nrow]
            q8 = q_ref[0, hh] * (ATT_HD ** -0.5)
            qh = jnp.concatenate([q8] * npat + [jnp.zeros((DEC_SROWS - npat * DEC_Q, ATT_HD), F32)],
                                 axis=0).astype(BF16)
            st["s"] = _dot(qh, kext.astype(BF16)) + bias_ref[...]

        def softmax(st=st):
            s = st.pop("s")
            st["m"] = jnp.max(s, axis=-1, keepdims=True)
            pexp = jnp.exp(s - st["m"])
            st["l"] = jnp.sum(pexp, axis=-1, keepdims=True)
            st["p"] = pexp.astype(BF16)

        def output(hh=hh, st=st):
            acc = _dot_nt(st.pop("p"), st.pop("v"))
            m, l = st.pop("m"), st.pop("l")
            mx = jnp.maximum(jnp.maximum(m[0:DEC_Q], m[DEC_Q:2 * DEC_Q]), m[2 * DEC_Q:3 * DEC_Q])
            num = jnp.zeros((DEC_Q, ATT_HD), F32)
            den = jnp.zeros((DEC_Q, 1), F32)
            for p in range(npat):
                w = jnp.exp(m[p * DEC_Q:(p + 1) * DEC_Q] - mx)
                num = num + w * acc[p * DEC_Q:(p + 1) * DEC_Q]
                den = den + w * l[p * DEC_Q:(p + 1) * DEC_Q]
            o_ref[0, hh] = num / den

        units.append((scores, softmax, output))
    return units


def _attn_kernel(q3_ref, kn_ref, vn_ref, kc_ref, vc_ref, bias_ref, qp_ref, kp_ref, vp_ref, *rest,
                 ts, t, rows_c, sub_steps, aliased):
    if aliased:
        rest = rest[2:]
    od_ref, ko_ref, vo_ref, op_ref, os_ref, ls_ref = rest
    sub = pl.program_id(0) % sub_steps
    prompt = _attn_prompt_units(qp_ref, kp_ref, vp_ref, os_ref, ls_ref, sub, t=t, blocks=t // ATT_BLK // sub_steps)
    decode = _attn_decode_units(q3_ref, kn_ref, vn_ref, kc_ref, vc_ref, bias_ref, od_ref, ko_ref, vo_ref, ts=ts)
    _run_skewed(_interleave(prompt, decode))

    @pl.when(sub == sub_steps - 1)
    def _():
        _attn_prompt_merge(op_ref, os_ref, ls_ref, t=t, rows_c=rows_c)


def _attn(q3, kn, vn, cache_kt, cache_vt, bias, layer, prev_out, ts, qp, kp, vp, rows_c=256):
    bs = q3.shape[0]
    nrow = cache_kt.shape[-1]
    b, t, _ = qp.shape
    ngrp = ATT_HEADS // DEC_HB
    npair = W_BR // LANES
    nsteps = bs * ngrp
    sub_steps = nsteps // (b * npair)
    assert sub_steps * b * npair == nsteps and t // ATT_BLK % sub_steps == 0
    cache = pl.BlockSpec((1, 1, DEC_HB, ATT_HD, nrow), lambda i: (layer, i // ngrp, i % ngrp, 0, 0))
    new = pl.BlockSpec((1, DEC_HB, DEC_Q, ATT_HD), lambda i: (i // ngrp, i % ngrp, 0, 0))
    prompt_map = lambda i: (i // sub_steps // npair, 0, i // sub_steps % npair)
    prompt = pl.BlockSpec((1, t, LANES), prompt_map)
    prompt_in = pl.BlockSpec((1, t, LANES), prompt_map, pipeline_mode=pl.Buffered(1))
    in_specs = [new, new, new,
                cache, cache, pl.BlockSpec(bias.shape, lambda i: (0, 0)), prompt_in, prompt_in, prompt_in]
    args = [q3, kn, vn, cache_kt, cache_vt, bias, qp, kp, vp]
    aliases = {}
    if prev_out is not None:
        in_specs += [pl.BlockSpec(memory_space=pl.ANY)] * 2
        aliases = {len(args): 1, len(args) + 1: 2}
        args += list(prev_out)
    return pl.pallas_call(
        functools.partial(_attn_kernel, ts=ts, t=t, rows_c=rows_c, sub_steps=sub_steps,
                          aliased=prev_out is not None),
        grid=(nsteps,),
        in_specs=in_specs,
        out_specs=[new, cache, cache, prompt],
        out_shape=[jax.ShapeDtypeStruct((bs, ATT_HEADS, DEC_Q, ATT_HD), F32),
                   jax.ShapeDtypeStruct(cache_kt.shape, F32), jax.ShapeDtypeStruct(cache_vt.shape, F32),
                   jax.ShapeDtypeStruct((b, t, W_BR), F32)],
        scratch_shapes=[pltpu.VMEM((3, t, LANES), F32), pltpu.VMEM((3, t, LANES), F32)],
        input_output_aliases=aliases,
        compiler_params=_params(("arbitrary",), ATT_VMEM_LIMIT),
        name="attn",
    )(*args)


def _decode_bias(nrow, ts):
    ri = jnp.arange(DEC_SROWS)
    pat, qt = ri // DEC_Q, ri % DEC_Q
    wins = jnp.array([w for w, _ in DIL_PATTERNS] + [0], jnp.int32)[pat]
    dils = jnp.array([d for _, d in DIL_PATTERNS] + [1], jnp.int32)[pat]
    real_row = (pat < len(DIL_PATTERNS)) & (qt < ts)
    ci = jnp.arange(nrow + LANES)
    key_real = ci < nrow + ts
    delta = (nrow + qt)[:, None] - ci[None, :]
    ok = (delta >= 0) & (delta % dils[:, None] == 0) & (delta <= wins[:, None]) & key_real[None, :]
    ok = ok | ~real_row[:, None]
    return jnp.where(ok, 0.0, NEG).astype(F32)


HG_TILE = 128
HG_PROMPT_TILES = 4


def _hgrn_kernel(q_ref, k_ref, g_ref, v_ref, s0_ref, ng_ref, *rest, seg, tiles, aliased):
    o_ref, so_ref, st_ref = rest[1:] if aliased else rest
    nstate = tiles * (HG_TILE // seg)
    t = pl.program_id(1)

    @pl.when(t == 0)
    def _():
        for j in range(nstate):
            for h in range(HG_HEADS):
                st_ref[j, h] = s0_ref[0, j, h].T

    live = [_hgrn_tile(q_ref, k_ref, g_ref, v_ref, ng_ref, o_ref, st_ref, n, seg=seg) for n in range(tiles)]
    while live:
        live = [gen for gen in live if next(gen, "done") != "done"]

    @pl.when(t == pl.num_programs(1) - 1)
    def _():
        for j in range(nstate):
            for h in range(HG_HEADS):
                so_ref[0, j, h] = st_ref[j, h].T


def _hgrn_tile(q_ref, k_ref, g_ref, v_ref, ng_ref, o_ref, st_ref, n, *, seg):
    c = HG_TILE
    nseg = c // seg
    q = q_ref[n]
    k = k_ref[n]
    g = g_ref[n]
    v = v_ref[n]
    row = lax.broadcasted_iota(jnp.int32, (c, c), 0)
    col = lax.broadcasted_iota(jnp.int32, (c, c), 1)
    rowi = lax.broadcasted_iota(jnp.int32, (c, W_BR), 0)
    seg_shift = seg.bit_length() - 1
    same_seg = (row >> seg_shift) == (col >> seg_shift)

    tri = jnp.where(jnp.logical_and(row >= col, same_seg), 1.0, 0.0).astype(BF16)
    g_hi = g.astype(BF16)
    r1 = g - g_hi.astype(F32)
    g_mid = r1.astype(BF16)
    g_lo = (r1 - g_mid.astype(F32)).astype(BF16)
    gc = (_dot(tri, g_hi) + _dot(tri, g_mid) + _dot(tri, g_lo)) * LOG2E

    def seg_last(x):
        x3 = x.reshape(nseg, seg, W_BR)
        return jnp.broadcast_to(x3[:, seg - 1:seg, :], x3.shape).reshape(c, W_BR)

    g_end = seg_last(gc)
    q_dec = (q * jnp.exp2(gc)).astype(BF16)
    k_dec = (k * jnp.exp2(g_end - gc)).astype(BF16)
    vb = v.astype(BF16)

    def head(x, h):
        return x[:, h * HG_DK:(h + 1) * HG_DK]

    qb = q.astype(BF16)
    kb = k.astype(BF16)
    a_acc = [jnp.where(row == col, _dot_nt(head(qb, h), head(kb, h)), 0.0) for h in range(HG_HEADS)]
    yield

    hlev = 1
    while hlev < seg:
        blk = 2 * hlev
        if hlev >= SUBLANES:
            nblk = c // blk
            g3 = gc.reshape(nblk, blk, W_BR)
            ref = g3[:, hlev - 1:hlev, :]
            k_lo = k.reshape(nblk, blk, W_BR)[:, 0:hlev] * jnp.exp2(ref - g3[:, 0:hlev])
            q_up = q.reshape(nblk, blk, W_BR)[:, hlev:blk] * jnp.exp2(g3[:, hlev:blk] - ref)
            zero = jnp.zeros((nblk, hlev, W_BR), F32)
            q_lv = jnp.concatenate([zero, q_up], axis=1).reshape(c, W_BR).astype(BF16)
            k_lv = jnp.concatenate([k_lo, zero], axis=1).reshape(c, W_BR).astype(BF16)
        else:
            if hlev == 4:
                g3 = gc.reshape(c // blk, blk, W_BR)
                ref = jnp.broadcast_to(g3[:, hlev - 1:hlev, :], g3.shape).reshape(c, W_BR)
            elif hlev == 1:
                ref = jnp.where((rowi & 1) == 1, pltpu.roll(gc, 1, 0), gc)
            else:
                pos = rowi & 3
                ref = jnp.where(pos == 0, pltpu.roll(gc, c - 1, 0),
                                jnp.where(pos == 1, gc,
                                          jnp.where(pos == 2, pltpu.roll(gc, 1, 0), pltpu.roll(gc, 2, 0))))
            x = jnp.exp2(-jnp.abs(gc - ref))
            upper = (rowi & (blk - 1)) >= hlev
            q_lv = jnp.where(upper, q * x, 0.0).astype(BF16)
            k_lv = jnp.where(upper, 0.0, k * x).astype(BF16)
        shift = blk.bit_length() - 1
        same_blk = (row >> shift) == (col >> shift)
        for h in range(HG_HEADS):
            a_acc[h] = a_acc[h] + jnp.where(same_blk, _dot_nt(head(q_lv, h), head(k_lv, h)), 0.0)
        hlev = blk
        yield

    d_end = jnp.exp2(g_end)
    for h in range(HG_HEADS):
        o_h = _dot(a_acc[h].astype(BF16), head(vb, h))
        vt = head(v, h).T.astype(BF16)
        kd = head(k_dec, h)
        inter = []
        for j in range(nseg):
            st = st_ref[n * nseg + j, h]
            inter.append(_dot_nt(head(q_dec, h)[j * seg:(j + 1) * seg, :], st.astype(BF16)))
            kd_j = kd if nseg == 1 else jnp.where((rowi[:, :HG_DK] >> seg_shift) == j, kd, jnp.zeros_like(kd))
            dj = head(d_end, h)[j * seg:j * seg + 1, :]
            st_ref[n * nseg + j, h] = st * dj + _dot(vt, kd_j)
        o_h = o_h + (inter[0] if nseg == 1 else jnp.concatenate(inter, axis=0))
        o_ref[n, :, h * HG_DV:(h + 1) * HG_DV] = _rms(o_h, head(ng_ref[...], h))
        yield


def _hgrn(qq, kk, lf, vv, s0, layer_in, norm_g, groups, steps, seg, depth, layer_out, prev_states, tiles=1):
    n = qq.shape[0]
    nseg = HG_TILE // seg
    nstate = tiles * nseg
    assert groups % tiles == 0
    tile = pl.BlockSpec((tiles, HG_TILE, W_BR), lambda gi, ti: (gi, ti, 0))
    state_in = pl.BlockSpec((1, nstate, HG_HEADS, HG_DK, HG_DV), lambda gi, ti: (layer_in, gi, 0, 0, 0))
    state_out = pl.BlockSpec((1, nstate, HG_HEADS, HG_DK, HG_DV), lambda gi, ti: (layer_out, gi, 0, 0, 0))
    in_specs = [tile, tile, tile, tile, state_in, pl.BlockSpec((1, W_BR), lambda gi, ti: (0, 0))]
    args = [a.reshape(groups, steps * HG_TILE, W_BR) for a in (qq, kk, lf, vv)] + [s0, norm_g]
    aliases = {}
    if prev_states is not None:
        in_specs.append(pl.BlockSpec(memory_space=pl.ANY))
        aliases = {len(args): 1}
        args.append(prev_states)
    o, states = pl.pallas_call(
        functools.partial(_hgrn_kernel, seg=seg, tiles=tiles, aliased=prev_states is not None),
        grid=(groups // tiles, steps),
        in_specs=in_specs,
        out_specs=[tile, state_out],
        out_shape=[jax.ShapeDtypeStruct((groups, steps * HG_TILE, W_BR), F32),
                   jax.ShapeDtypeStruct((depth, groups * nseg, HG_HEADS, HG_DK, HG_DV), F32)],
        scratch_shapes=[pltpu.VMEM((nstate, HG_HEADS, HG_DV, HG_DK), F32)],
        input_output_aliases=aliases,
        compiler_params=_params(("parallel", "arbitrary")),
        name="hgrn",
    )(*args)
    return o.reshape(n, W_BR), states


def _merge_kernel(x_ref, ya_ref, yb_ref, yc_ref, g_ref, w4_ref, wbr_ref, bg_ref, wo_ref, fg_ref, o_ref, *, final):
    x = x_ref[...]
    hb = _rms(x, g_ref[...]).astype(BF16)
    gate0 = 3 * W_BR
    merged = jnp.zeros(x.shape, F32)
    for p, y_ref in enumerate((ya_ref, yb_ref, yc_ref)):
        z = _dot(hb, w4_ref[:, p * W_BR:(p + 1) * W_BR])
        ys = (y_ref[...] * (z * _sigmoid(z))).astype(BF16)
        br = _dot(ys, wbr_ref[p])
        gl = _dot(hb, w4_ref[:, gate0 + p * D_MODEL:gate0 + (p + 1) * D_MODEL]) + bg_ref[p:p + 1, :]
        merged = merged + _sigmoid(gl) * br
    out = x + _dot(merged.astype(BF16), wo_ref[...])
    if final:
        out = _rms(out, fg_ref[...])
    o_ref[...] = out


def _merge(x2d, ya, yb, yc, norm_g, w4, wbr, b_gate, wo, final_g, final, tm):
    n = x2d.shape[0]
    tm = min(tm, n)
    row = lambda i: (i, 0)
    fixed = lambda i: (0, 0)
    ysp = pl.BlockSpec((tm, W_BR), row)
    return pl.pallas_call(
        functools.partial(_merge_kernel, final=final),
        grid=(n // tm,),
        in_specs=[pl.BlockSpec((tm, D_MODEL), row), ysp, ysp, ysp, pl.BlockSpec((1, D_MODEL), fixed),
                  pl.BlockSpec(w4.shape, fixed, pipeline_mode=pl.Buffered(1)),
                  pl.BlockSpec(wbr.shape, lambda i: (0, 0, 0), pipeline_mode=pl.Buffered(1)),
                  pl.BlockSpec(b_gate.shape, fixed), pl.BlockSpec(wo.shape, fixed, pipeline_mode=pl.Buffered(1)),
                  pl.BlockSpec((1, D_MODEL), fixed)],
        out_specs=pl.BlockSpec((tm, D_MODEL), row),
        out_shape=jax.ShapeDtypeStruct((n, D_MODEL), F32),
        compiler_params=_params(("parallel",)),
        name="merge",
    )(x2d, ya, yb, yc, norm_g, w4, wbr, b_gate, wo, final_g)


def _rope_tables(pos):
    half = ROPE_DIM // 2
    freqs = jnp.power(jnp.float32(ROPE_THETA), -jnp.arange(half, dtype=F32) / half)
    ang = pos.astype(F32)[:, None] * freqs[None, :]
    cos, sin = jnp.cos(ang), jnp.sin(ang)
    rows = pos.shape[0]
    ones = jnp.ones((rows, ATT_HD - ROPE_DIM), F32)
    zeros = jnp.zeros((rows, ATT_HD - ROPE_DIM), F32)
    zh = jnp.zeros((rows, half), F32)
    c_head = jnp.concatenate([cos, cos, ones], axis=1)
    sa_head = jnp.concatenate([-sin, zh, zeros], axis=1)
    sb_head = jnp.concatenate([zh, sin, zeros], axis=1)
    rep = LANES // ATT_HD
    return tuple(jnp.tile(x, (1, rep)) for x in (c_head, sa_head, sb_head))


def kernel(x_prompt, x_sample, cache_conv, cache_win_k, cache_win_v, state_hgrn, norm_g, w_in, conv_w, conv_b, ln_a_g, ln_a_b, hg_lb_param, hg_norm_g, w_branch, b_gate, w_out, final_norm_g):
    b, t, d = x_prompt.shape
    bs, ts, _ = x_sample.shape
    depth = w_in.shape[0]
    nrow = cache_win_k.shape[2]
    tm = 512

    lb_all = jnp.cumsum(jax.nn.softmax(hg_lb_param.astype(F32), axis=0), axis=0)
    lb_all = lb_all - lb_all[:1]

    rope_p = _rope_tables(jnp.arange(t, dtype=jnp.int32))
    rope_s = _rope_tables(PAST_LEN + jnp.arange(bs * ts, dtype=jnp.int32) % ts)
    bias = _decode_bias(nrow, ts)
    zero_state = jnp.zeros((1, b, HG_HEADS, HG_DK, HG_DV), F32)
    cache_kt = jnp.transpose(cache_win_k, (0, 1, 3, 4, 2))
    cache_vt = jnp.transpose(cache_win_v, (0, 1, 3, 4, 2))
    cache_conv_t = jnp.transpose(cache_conv, (0, 2, 1, 3))

    def head_major(a):
        a = jnp.transpose(a.reshape(bs, ts, ATT_HEADS, ATT_HD), (0, 2, 1, 3))
        return jnp.pad(a, ((0, 0), (0, 0), (0, DEC_Q - ts), (0, 0)))

    xp = x_prompt.reshape(b * t, d)
    xs = x_sample.reshape(bs * ts, d)
    conv_p, conv_s = [], []
    s_p = s_s = kvt_p = None
    kv_out = None
    seg_s = 16
    for l in range(depth):
        wl = w_in[l]
        w1 = jnp.concatenate([wl[:, 0:2 * W_BR], wl[:, 3 * W_BR:6 * W_BR], wl[:, 7 * W_BR:10 * W_BR]], axis=1).astype(BF16)
        w4 = jnp.concatenate([wl[:, 2 * W_BR:3 * W_BR], wl[:, 6 * W_BR:7 * W_BR], wl[:, 10 * W_BR:11 * W_BR],
                              wl[:, 11 * W_BR:]], axis=1).astype(BF16)
        wbr = w_branch[l].astype(BF16)
        wo = w_out[l].astype(BF16)
        lb = lb_all[l]
        lbp = jnp.stack([jnp.log(lb), jnp.log1p(-lb), 1.0 - lb])
        ng = norm_g[l][None]
        cw, cb, lg, lbias = conv_w[l], conv_b[l][None], ln_a_g[l][None], ln_a_b[l][None]
        hng = hg_norm_g[l][None]

        up, qp, kp, vp, qqp, kkp, lfp, vvp, kt_p, vt_p = _inproj(
            xp, ng, w1, *rope_p, lbp, tm, window=(t, min(WIN_MAX, t), depth, l, kvt_p))
        kvt_p = (kt_p, vt_p)
        us, qs, ks, vs, qqs, kks, lfs, vvs = _inproj(xs, ng, w1, *rope_s, lbp, tm)

        up3 = up.reshape(b, t, W_BR)
        ya_p = _conv_prompt(up3, cw, cb, lg, lbias).reshape(b * t, W_BR)
        conv_p.append(up3[:, t - (CONV_W - 1):])
        us_t = jnp.transpose(us.reshape(bs, ts, W_BR), (1, 0, 2))
        ya_st, nc = _conv_decode(cache_conv_t, l, us_t, cw, cb, lg, lbias)
        ya_s = jnp.transpose(ya_st, (1, 0, 2)).reshape(bs * ts, W_BR)
        conv_s.append(nc)

        qp3, kp3, vp3 = (a.reshape(b, t, W_BR) for a in (qp, kp, vp))
        q8, kn8, vn8 = (head_major(a) for a in (qs, ks, vs))
        of, k_out, v_out, yb_p3 = _attn(q8, kn8, vn8, cache_kt, cache_vt, bias, l, kv_out, ts, qp3, kp3, vp3)
        kv_out = (k_out, v_out)
        yb_p = yb_p3.reshape(b * t, W_BR)
        yb_s = jnp.transpose(of[:, :, :ts], (0, 2, 1, 3)).reshape(bs * ts, W_BR)

        yc_p, s_p = _hgrn(qqp, kkp, lfp, vvp, zero_state, 0, hng, b, t // HG_TILE, HG_TILE, depth, l, s_p,
                          tiles=HG_PROMPT_TILES if b % HG_PROMPT_TILES == 0 else 1)

        def pad_seg(a):
            return jnp.pad(a.reshape(bs, ts, W_BR), ((0, 0), (0, seg_s - ts), (0, 0))).reshape(bs * seg_s, W_BR)

        yc_s16, s_s = _hgrn(pad_seg(qqs), pad_seg(kks), pad_seg(lfs), pad_seg(vvs), state_hgrn, l, hng,
                            bs * seg_s // HG_TILE, 1, seg_s, depth, l, s_s)
        yc_s = yc_s16.reshape(bs, seg_s, W_BR)[:, :ts].reshape(bs * ts, W_BR)

        final = l == depth - 1
        fg = final_norm_g[None]
        xp = _merge(xp, ya_p, yb_p, yc_p, ng, w4, wbr, b_gate[l], wo, fg, final, 2 * tm)
        xs = _merge(xs, ya_s, yb_s, yc_s, ng, w4, wbr, b_gate[l], wo, fg, final, tm)

    return (xp.reshape(b, t, d), xs.reshape(bs, ts, d),
            jnp.stack(conv_p), jnp.transpose(kvt_p[0], (0, 1, 4, 2, 3)), jnp.transpose(kvt_p[1], (0, 1, 4, 2, 3)), s_p,
            jnp.transpose(jnp.stack(conv_s), (0, 2, 1, 3)),
            jnp.transpose(kv_out[0], (0, 1, 4, 2, 3)), jnp.transpose(kv_out[1], (0, 1, 4, 2, 3)), s_s)
```
